```python
import math
import jax
import jax.numpy as jnp
from jax import lax
import numpy as np

D_MODEL = 1024
BATCH = 32
SEQ = 2048
DEPTH = 4

N_MIXERS = 2
NORM_EPS = 1e-6

D_FF = 2816
MACARON_WEIGHT = 0.5

A_PATTERNS = ((128, 1), (512, 4), (2048, 16))
A_GROUPS = len(A_PATTERNS)
A_HEADS = 8
A_HEAD_DIM = 64
A_GROUP_WIDTH = A_HEADS * A_HEAD_DIM
A_IN_WIDTH = A_GROUPS * 3 * A_GROUP_WIDTH
A_BLOCK = 128
NEG_INF = -1e30

NUM_BUCKETS = 32
MAX_DISTANCE = 2048

B_HEADS = 8
B_HEAD_DIM = 128
B_WIDTH = B_HEADS * B_HEAD_DIM
B_IN_WIDTH = 4 * B_WIDTH + 2 * B_HEADS
B_CONV = 4
B_CHUNK = 64

N_A_LAYERS = len(range(0, DEPTH, N_MIXERS))
N_B_LAYERS = len(range(1, DEPTH, N_MIXERS))

kernel_name = "hybrid_dilated_attn_gated_deltanet_macaron"


def rms_norm(x, g):
    xf = x.astype(jnp.float32)
    y = xf * lax.rsqrt(jnp.mean(xf * xf, axis=-1, keepdims=True) + NORM_EPS)
    return (y * g.astype(jnp.float32)).astype(x.dtype)


def swiglu(h, w_gate, w_up, w_down):
    return (jax.nn.silu(h @ w_gate) * (h @ w_up)) @ w_down


def t5_bucket(distance):
    max_exact = NUM_BUCKETS // 2
    n = distance.astype(jnp.float32)
    large = max_exact + (jnp.log(jnp.maximum(n, 1.0) / max_exact)
                         / math.log(MAX_DISTANCE / max_exact) * (NUM_BUCKETS - max_exact))
    large = jnp.minimum(large.astype(jnp.int32), NUM_BUCKETS - 1)
    return jnp.where(distance < max_exact, distance, large)


def dilated_group_attention(q, k, v, bias_h, window, dilation):
    B, S, H, dh = q.shape
    L = S // dilation
    nb = -(-L // A_BLOCK)
    Lp = nb * A_BLOCK
    steps = window // dilation

    def to_blocks(t):
        t = t.reshape(B, L, dilation, H, dh).transpose(0, 2, 1, 3, 4)
        t = jnp.pad(t, ((0, 0), (0, 0), (0, Lp - L), (0, 0), (0, 0)))
        return t.reshape(B, dilation, nb, A_BLOCK, H, dh)

    def with_prev(t):
        prev = jnp.pad(t, ((0, 0), (0, 0), (1, 0), (0, 0), (0, 0), (0, 0)))[:, :, :-1]
        return jnp.concatenate([prev, t], axis=3)

    qb = to_blocks(q)
    kk = with_prev(to_blocks(k))
    vv = with_prev(to_blocks(v))

    s = jnp.einsum('bcnqhe,bcnkhe->bcnhqk', qb, kk) * (A_HEAD_DIM ** -0.5)
    q_loc = jnp.arange(A_BLOCK)[:, None]
    k_loc = jnp.arange(2 * A_BLOCK)[None, :]
    rel = q_loc + A_BLOCK - k_loc
    bias = bias_h.astype(jnp.float32)[t5_bucket(jnp.maximum(rel, 0) * dilation)]
    s = s + bias.transpose(2, 0, 1)
    k_idx = jnp.arange(nb)[:, None] * A_BLOCK + k_loc - A_BLOCK
    valid = ((rel >= 0) & (rel <= steps))[None] & (k_idx >= 0)[:, None, :]
    s = jnp.where(valid[:, None], s, NEG_INF)

    m = jnp.max(s, axis=-1, keepdims=True)
    p = jnp.exp(s - m)
    den = jnp.sum(p, axis=-1)
    o = jnp.einsum('bcnhqk,bcnkhe->bcnqhe', p, vv) / den.transpose(0, 1, 2, 4, 3)[..., None]
    lse = m[..., 0] + jnp.log(den)

    o = o.reshape(B, dilation, Lp, H, dh)[:, :, :L].transpose(0, 2, 1, 3, 4).reshape(B, S, H, dh)
    lse = lse.transpose(0, 1, 2, 4, 3).reshape(B, dilation, Lp, H)[:, :, :L]
    lse = lse.transpose(0, 2, 1, 3).reshape(B, S, H)
    return o, lse


def dilated_attention_mixer(h, w_in, rel_bias, w_out):
    B, S, _ = h.shape
    proj = (h @ w_in).astype(jnp.float32).reshape(B, S, A_GROUPS, 3, A_HEADS, A_HEAD_DIM)
    outs, lses = [], []
    for g, (window, dilation) in enumerate(A_PATTERNS):
        o, lse = dilated_group_attention(proj[:, :, g, 0], proj[:, :, g, 1], proj[:, :, g, 2],
                                         rel_bias[:, g * A_HEADS:(g + 1) * A_HEADS],
                                         window, dilation)
        outs.append(o)
        lses.append(lse)
    alpha = jax.nn.softmax(jnp.stack(lses, 0), axis=0)
    o = jnp.einsum('gbsh,gbshe->bshe', alpha, jnp.stack(outs, 0))
    return o.reshape(B, S, A_GROUP_WIDTH).astype(h.dtype) @ w_out


def causal_depthwise_conv(x, w):
    K, C = w.shape
    return lax.conv_general_dilated(x, w[:, None, :].astype(x.dtype), window_strides=(1,),
                                    padding=[(K - 1, 0)],
                                    dimension_numbers=('NWC', 'WIO', 'NWC'),
                                    feature_group_count=C)


def l2_normalize(t):
    return t * lax.rsqrt(jnp.sum(t * t, axis=-1, keepdims=True) + NORM_EPS)


def chunk_gated_delta_rule(q, k, v, g, beta):
    B, S, H, dk = q.shape
    dv = v.shape[-1]
    C = B_CHUNK
    N = S // C

    def chunks(t):
        if t.ndim == 4:
            return t.reshape(B, N, C, H, t.shape[-1]).transpose(0, 3, 1, 2, 4)
        return t.reshape(B, N, C, H).transpose(0, 3, 1, 2)

    q, k, v, g, beta = map(chunks, (q, k, v, g, beta))
    gc = jnp.cumsum(g, axis=-1)
    idx = jnp.arange(C)
    tril = idx[:, None] >= idx[None, :]
    strict = idx[:, None] > idx[None, :]
    decay = jnp.exp(jnp.where(tril, gc[..., :, None] - gc[..., None, :], -jnp.inf))

    kb = k * beta[..., None]
    lower = jnp.where(strict, jnp.einsum('bhnid,bhnjd->bhnij', kb, k) * decay, 0.0)
    t_sys = jnp.eye(C, dtype=jnp.float32) + lower
    u = lax.linalg.triangular_solve(t_sys, v * beta[..., None], left_side=True, lower=True)
    w = lax.linalg.triangular_solve(t_sys, kb * jnp.exp(gc)[..., None], left_side=True, lower=True)

    attn = jnp.einsum('bhnid,bhnjd->bhnij', q, k) * decay
    q_dec = q * jnp.exp(gc)[..., None]
    k_tail = k * jnp.exp(gc[..., -1:] - gc)[..., None]
    chunk_dec = jnp.exp(gc[..., -1])

    def step(state, inp):
        u_n, w_n, attn_n, qd_n, kt_n, cd_n = inp
        v_new = u_n - jnp.einsum('bhcd,bhde->bhce', w_n, state)
        o_n = (jnp.einsum('bhcd,bhde->bhce', qd_n, state)
               + jnp.einsum('bhij,bhje->bhie', attn_n, v_new))
        state = state * cd_n[..., None, None] + jnp.einsum('bhcd,bhce->bhde', kt_n, v_new)
        return state, o_n

    xs = tuple(jnp.moveaxis(t, 2, 0) for t in (u, w, attn, q_dec, k_tail, chunk_dec))
    state0 = jnp.zeros((B, H, dk, dv), jnp.float32)
    _, o = lax.scan(step, state0, xs)
    return o.transpose(1, 0, 3, 2, 4).reshape(B, S, H, dv)


def gated_deltanet_mixer(h, w_in, conv_w, a_log, dt_bias, norm_w, w_out):
    B, S, _ = h.shape
    proj = h @ w_in
    qkv, z, a, b = jnp.split(proj, [3 * B_WIDTH, 4 * B_WIDTH, 4 * B_WIDTH + B_HEADS], axis=-1)
    qkv = jax.nn.silu(causal_depthwise_conv(qkv, conv_w).astype(jnp.float32))
    q, k, v = jnp.split(qkv, 3, axis=-1)
    q = l2_normalize(q.reshape(B, S, B_HEADS, B_HEAD_DIM)) * (B_HEAD_DIM ** -0.5)
    k = l2_normalize(k.reshape(B, S, B_HEADS, B_HEAD_DIM))
    v = v.reshape(B, S, B_HEADS, B_HEAD_DIM)
    beta = jax.nn.sigmoid(b.astype(jnp.float32))
    g = -jnp.exp(a_log.astype(jnp.float32)) * jax.nn.softplus(a.astype(jnp.float32)
                                                              + dt_bias.astype(jnp.float32))
    o = chunk_gated_delta_rule(q, k, v, g, beta)
    o = rms_norm(o, norm_w) * jax.nn.silu(z.astype(jnp.float32).reshape(B, S, B_HEADS, B_HEAD_DIM))
    return o.reshape(B, S, B_WIDTH).astype(h.dtype) @ w_out


def setup_inputs(seed: int = 0) -> dict:
    key = jax.random.key(seed)
    ks = jax.random.split(key, 16)
    f32 = jnp.float32

    def nrm(k, shape, scale):
        return jax.random.normal(k, shape, f32) * scale

    x = nrm(ks[0], (BATCH, SEQ, D_MODEL), 1.0)
    norm_g = 1.0 + nrm(ks[1], (DEPTH, 3, D_MODEL), 0.02)
    ffn_w_gate = nrm(ks[2], (DEPTH, 2, D_MODEL, D_FF), D_MODEL ** -0.5)
    ffn_w_up = nrm(ks[3], (DEPTH, 2, D_MODEL, D_FF), D_MODEL ** -0.5)
    ffn_w_down = nrm(ks[4], (DEPTH, 2, D_FF, D_MODEL), D_FF ** -0.5)
    rel_bias = nrm(ks[5], (NUM_BUCKETS, A_GROUPS * A_HEADS), 0.5)
    a_w_in = nrm(ks[6], (N_A_LAYERS, D_MODEL, A_IN_WIDTH), D_MODEL ** -0.5)
    a_w_out = nrm(ks[7], (N_A_LAYERS, A_GROUP_WIDTH, D_MODEL), A_GROUP_WIDTH ** -0.5)
    b_w_in = nrm(ks[8], (N_B_LAYERS, D_MODEL, B_IN_WIDTH), D_MODEL ** -0.5)
    b_conv_w = nrm(ks[9], (N_B_LAYERS, B_CONV, 3 * B_WIDTH), B_CONV ** -0.5)
    b_a_log = jnp.log(jax.random.uniform(ks[10], (N_B_LAYERS, B_HEADS), f32, 1.0, 16.0))
    dt = jnp.exp(jax.random.uniform(ks[11], (N_B_LAYERS, B_HEADS), f32,
                                    math.log(1e-3), math.log(1e-1)))
    b_dt_bias = dt + jnp.log(-jnp.expm1(-dt))
    b_norm_w = 1.0 + nrm(ks[12], (N_B_LAYERS, B_HEAD_DIM), 0.02)
    b_w_out = nrm(ks[13], (N_B_LAYERS, B_WIDTH, D_MODEL), B_WIDTH ** -0.5)
    final_g = 1.0 + nrm(ks[14], (D_MODEL,), 0.02)
    return {"x": x, "norm_g": norm_g, "ffn_w_gate": ffn_w_gate, "ffn_w_up": ffn_w_up,
            "ffn_w_down": ffn_w_down, "rel_bias": rel_bias, "a_w_in": a_w_in,
            "a_w_out": a_w_out, "b_w_in": b_w_in, "b_conv_w": b_conv_w,
            "b_a_log": b_a_log, "b_dt_bias": b_dt_bias, "b_norm_w": b_norm_w,
            "b_w_out": b_w_out, "final_g": final_g}


def reference(x, norm_g, ffn_w_gate, ffn_w_up, ffn_w_down, rel_bias, a_w_in, a_w_out,
              b_w_in, b_conv_w, b_a_log, b_dt_bias, b_norm_w, b_w_out, final_g):
    for i in range(DEPTH):
        j = i // N_MIXERS
        x = x + MACARON_WEIGHT * swiglu(rms_norm(x, norm_g[i, 0]),
                                        ffn_w_gate[i, 0], ffn_w_up[i, 0], ffn_w_down[i, 0])
        h = rms_norm(x, norm_g[i, 1])
        if i % N_MIXERS == 0:
            x = x + dilated_attention_mixer(h, a_w_in[j], rel_bias, a_w_out[j])
        else:
            x = x + gated_deltanet_mixer(h, b_w_in[j], b_conv_w[j], b_a_log[j], b_dt_bias[j],
                                         b_norm_w[j], b_w_out[j])
        x = x + MACARON_WEIGHT * swiglu(rms_norm(x, norm_g[i, 2]),
                                        ffn_w_gate[i, 1], ffn_w_up[i, 1], ffn_w_down[i, 1])
    return rms_norm(x, final_g)
```

```python
import functools
import math

import numpy as np
import jax
import jax.numpy as jnp
from jax import lax
from jax.experimental import pallas as pl
from jax.experimental.pallas import tpu as pltpu

F32 = jnp.float32
BF16 = jnp.bfloat16

D_MODEL = 1024
SEQ = 2048
D_FF = 2816
NORM_EPS = 1e-6
MACARON_WEIGHT = 0.5

A_PATTERNS = ((128, 1), (512, 4), (2048, 16))
A_GROUPS = 3
A_HEADS = 8
A_HEAD_DIM = 64
A_GROUP_WIDTH = A_HEADS * A_HEAD_DIM
A_IN_WIDTH = A_GROUPS * 3 * A_GROUP_WIDTH
A_BLOCK = 128
A_PAIR = 2 * A_HEAD_DIM
NEG_INF = -1e30
NUM_BUCKETS = 32
MAX_DISTANCE = 2048

B_HEADS = 8
B_HEAD_DIM = 128
B_WIDTH = B_HEADS * B_HEAD_DIM
B_CONV = 4
B_CHUNK = 64
N_CHUNKS = SEQ // B_CHUNK

VMEM_LIMIT = 56 * 1024 * 1024

NT_DIMS = (((1,), (1,)), ((), ()))
TN_DIMS = (((0,), (0,)), ((), ()))


def _const_spec(shape):
    zeros = (0,) * len(shape)
    return pl.BlockSpec(shape, lambda *_: zeros, pipeline_mode=pl.Buffered(1))


def _params(n_axes):
    return pltpu.CompilerParams(dimension_semantics=("parallel",) * n_axes,
                                vmem_limit_bytes=VMEM_LIMIT)


def _rms(x, g):
    ms = jnp.mean(x * x, axis=-1, keepdims=True)
    return x * lax.rsqrt(ms + NORM_EPS) * g


def _silu(x):
    return x * jax.nn.sigmoid(x)


def _softplus(x):
    return jnp.maximum(x, 0.0) + jnp.log1p(jnp.exp(-jnp.abs(x)))


def _dot_hi(a, b):
    return jnp.dot(a, b, preferred_element_type=F32, precision=lax.Precision.HIGHEST)


def _ffn_kernel(x_ref, g_ref, wg_ref, wu_ref, wd_ref, *rest, final):
    o_ref = rest[-1]
    x = x_ref[...]
    h = _rms(x, g_ref[...]).astype(BF16)
    gate = jnp.dot(h, wg_ref[...], preferred_element_type=F32)
    up = jnp.dot(h, wu_ref[...], preferred_element_type=F32)
    a = (_silu(gate) * up).astype(BF16)
    y = x + MACARON_WEIGHT * jnp.dot(a, wd_ref[...], preferred_element_type=F32)
    if final:
        y = _rms(y, rest[0][...])
    o_ref[...] = y


def _ffn(x, g, wg, wu, wd, final_g=None, tm=512):
    t = x.shape[0]
    row = pl.BlockSpec((tm, D_MODEL), lambda i: (i, 0))
    in_specs = [row, _const_spec((1, D_MODEL)), _const_spec((D_MODEL, D_FF)),
                _const_spec((D_MODEL, D_FF)), _const_spec((D_FF, D_MODEL))]
    args = [x, g, wg, wu, wd]
    if final_g is not None:
        in_specs.append(_const_spec((1, D_MODEL)))
        args.append(final_g)
    return pl.pallas_call(
        functools.partial(_ffn_kernel, final=final_g is not None),
        grid=(t // tm,), in_specs=in_specs, out_specs=row,
        out_shape=jax.ShapeDtypeStruct((t, D_MODEL), F32),
        compiler_params=_params(1), name="ffn")(*args)


def _proj_res_kernel(x_ref, y_ref, w_ref, o_ref):
    o_ref[...] = x_ref[...] + jnp.dot(y_ref[...], w_ref[...], preferred_element_type=F32)


def _proj_res(x, y, w, tm=1024):
    t, k = y.shape
    return pl.pallas_call(
        _proj_res_kernel, grid=(t // tm,),
        in_specs=[pl.BlockSpec((tm, D_MODEL), lambda i: (i, 0)),
                  pl.BlockSpec((tm, k), lambda i: (i, 0)),
                  _const_spec((k, D_MODEL))],
        out_specs=pl.BlockSpec((tm, D_MODEL), lambda i: (i, 0)),
        out_shape=jax.ShapeDtypeStruct((t, D_MODEL), F32),
        compiler_params=_params(1), name="proj_res")(x, y, w)


def _a_in_kernel(x_ref, g_ref, w_ref, o_ref):
    h = _rms(x_ref[...], g_ref[...]).astype(BF16)
    o_ref[...] = jnp.dot(h, w_ref[...], preferred_element_type=F32)


def _a_in(x, g, w, tm=512):
    t = x.shape[0]
    return pl.pallas_call(
        _a_in_kernel, grid=(t // tm,),
        in_specs=[pl.BlockSpec((tm, D_MODEL), lambda i: (i, 0)),
                  _const_spec((1, D_MODEL)), _const_spec((D_MODEL, A_IN_WIDTH))],
        out_specs=pl.BlockSpec((tm, A_IN_WIDTH), lambda i: (i, 0)),
        out_shape=jax.ShapeDtypeStruct((t, A_IN_WIDTH), F32),
        compiler_params=_params(1), name="a_in")(x, g, w)


def _t5_bucket_np(distance):
    max_exact = NUM_BUCKETS // 2
    n = distance.astype(np.float32)
    large = np.float32(max_exact) + (
        np.log(np.maximum(n, np.float32(1.0)) / np.float32(max_exact))
        / np.float32(math.log(MAX_DISTANCE / max_exact)) * np.float32(NUM_BUCKETS - max_exact))
    large = np.minimum(large.astype(np.int32), NUM_BUCKETS - 1)
    return np.where(distance < max_exact, distance, large).astype(np.int32)


def _a_bias_tables(rel_bias):
    q_loc = np.arange(A_BLOCK)[:, None]
    k_loc = np.arange(2 * A_BLOCK)[None, :]
    rel = q_loc + A_BLOCK - k_loc
    tabs = []
    for g, (window, dilation) in enumerate(A_PATTERNS):
        steps = window // dilation
        bucket = _t5_bucket_np(np.maximum(rel, 0) * dilation)
        valid = (rel >= 0) & (rel <= steps)
        b = rel_bias.astype(F32)[:, g * A_HEADS:(g + 1) * A_HEADS][bucket]
        tabs.append(jnp.where(valid[..., None], b, NEG_INF).transpose(2, 0, 1))
    return jnp.stack(tabs)


def _a_unit(q_ref, k_ref, v_ref, bias_ref, g, o_scr, l_scr, row0, dilation, first):
    def rows(start, n):
        if dilation == 1:
            return pl.ds(start, n)
        return pl.ds(start, n, stride=dilation)

    nk = A_BLOCK if first else 2 * A_BLOCK
    k0 = row0 if first else row0 - A_BLOCK * dilation
    q = q_ref[rows(row0, A_BLOCK), :] * (A_HEAD_DIM ** -0.5)
    k = k_ref[rows(k0, nk), :].astype(BF16)
    v = v_ref[rows(k0, nk), :]
    head0 = lax.broadcasted_iota(jnp.int32, (1, A_PAIR), 1) < A_HEAD_DIM
    res, mx = [], []
    for h in range(2):
        mine = head0 if h == 0 else jnp.logical_not(head0)
        qh = jnp.where(mine, q, 0.0).astype(BF16)
        s = lax.dot_general(qh, k, NT_DIMS, preferred_element_type=F32)
        if first:
            s = s + bias_ref[g, h, :, A_BLOCK:]
        else:
            s = s + bias_ref[g, h]
        m = jnp.max(s, axis=-1, keepdims=True)
        p = jnp.exp(s - m).astype(BF16)
        ve = jnp.where(mine, v, 1.0).astype(BF16)
        res.append(jnp.dot(p, ve, preferred_element_type=F32))
        mx.append(m)
    acc = jnp.where(head0, res[0], res[1])
    den = pltpu.roll(jnp.where(head0, res[1], res[0]), A_HEAD_DIM, 1)
    lse = jnp.where(head0, mx[0], mx[1]) + jnp.log(den)
    o_scr[g, rows(row0, A_BLOCK), :] = acc / den
    l_scr[g, rows(row0, A_BLOCK), :] = lse


def _a_attn_kernel(q0, k0, v0, q1, k1, v1, q2, k2, v2, bias_ref, o_ref, o_scr, l_scr):
    qkv = ((q0, k0, v0), (q1, k1, v1), (q2, k2, v2))
    for g, (_, dilation) in enumerate(A_PATTERNS):
        q_ref, k_ref, v_ref = qkv[g]
        nb = SEQ // dilation // A_BLOCK
        unit = functools.partial(_a_unit, q_ref, k_ref, v_ref, bias_ref, g, o_scr, l_scr,
                                 dilation=dilation)

        def per_subseq(c, carry, unit=unit, nb=nb, dilation=dilation):
            unit(row0=c, first=True)

            def per_block(i, carry2):
                unit(row0=c + i * (A_BLOCK * dilation), first=False)
                return carry2

            if nb > 1:
                lax.fori_loop(1, nb, per_block, 0)
            return carry

        if dilation == 1:
            per_subseq(0, 0)
        else:
            lax.fori_loop(0, dilation, per_subseq, 0)

    rows_per_step = 256

    def merge(r, carry):
        sl = pl.ds(pl.multiple_of(r * rows_per_step, rows_per_step), rows_per_step)
        l0, l1, l2 = l_scr[0, sl, :], l_scr[1, sl, :], l_scr[2, sl, :]
        m = jnp.maximum(jnp.maximum(l0, l1), l2)
        w0, w1, w2 = jnp.exp(l0 - m), jnp.exp(l1 - m), jnp.exp(l2 - m)
        num = w0 * o_scr[0, sl, :] + w1 * o_scr[1, sl, :] + w2 * o_scr[2, sl, :]
        o_ref[sl, :] = (num / (w0 + w1 + w2)).astype(o_ref.dtype)
        return carry

    lax.fori_loop(0, SEQ // rows_per_step, merge, 0)


def _a_attn(proj, bias):
    b = proj.shape[0]
    n_pairs = A_GROUP_WIDTH // A_PAIR
    in_specs = []
    for g in range(A_GROUPS):
        for r in range(3):
            col = (g * 3 + r) * n_pairs
            in_specs.append(pl.BlockSpec((None, SEQ, A_PAIR),
                                         lambda bi, hp, col=col: (bi, 0, col + hp)))
    in_specs.append(pl.BlockSpec((A_GROUPS, 2, A_BLOCK, 2 * A_BLOCK), lambda bi, hp: (0, hp, 0, 0)))
    return pl.pallas_call(
        _a_attn_kernel, grid=(b, n_pairs), in_specs=in_specs,
        out_specs=pl.BlockSpec((None, SEQ, A_PAIR), lambda bi, hp: (bi, 0, hp)),
        out_shape=jax.ShapeDtypeStruct((b, SEQ, A_GROUP_WIDTH), BF16),
        scratch_shapes=[pltpu.VMEM((A_GROUPS, SEQ, A_PAIR), F32),
                        pltpu.VMEM((A_GROUPS, SEQ, A_PAIR), F32)],
        compiler_params=_params(2), name="a_attn")(*([proj] * 9), bias)


B_IN_TILE = 512
B_PREV = 8


def _b_in_kernel(x_ref, xp_ref, g_ref, wqkv_ref, wz_ref, wab_ref, wabt_ref, conv_ref,
                 alog_ref, dtb_ref, alogt_ref, dtbt_ref,
                 qkv_ref, z_ref, gates_ref, gct_ref, pre_scr):
    tiles_per_seq = SEQ // B_IN_TILE
    i = pl.program_id(0)
    g = g_ref[...]
    h = _rms(x_ref[...], g).astype(BF16)
    hp = _rms(xp_ref[...], g).astype(BF16)
    keep = jnp.where(i % tiles_per_seq == 0, 0.0, 1.0)
    pre_scr[pl.ds(0, B_PREV), :] = keep * jnp.dot(hp, wqkv_ref[...], preferred_element_type=F32)
    pre_scr[pl.ds(B_PREV, B_IN_TILE), :] = jnp.dot(h, wqkv_ref[...], preferred_element_type=F32)
    z_ref[...] = jnp.dot(h, wz_ref[...], preferred_element_type=F32)

    conv = conv_ref[B_CONV - 1:B_CONV, :] * pre_scr[pl.ds(B_PREV, B_IN_TILE), :]
    for j in range(B_CONV - 1):
        conv = conv + conv_ref[j:j + 1, :] * pre_scr[pl.ds(B_PREV - (B_CONV - 1) + j, B_IN_TILE), :]
    act = _silu(conv)
    for hd in range(2 * B_HEADS):
        sl = slice(hd * B_HEAD_DIM, (hd + 1) * B_HEAD_DIM)
        t = act[:, sl]
        n = t * lax.rsqrt(jnp.sum(t * t, axis=-1, keepdims=True) + NORM_EPS)
        if hd < B_HEADS:
            n = n * (B_HEAD_DIM ** -0.5)
        qkv_ref[:, sl] = n
    qkv_ref[:, 2 * B_WIDTH:] = act[:, 2 * B_WIDTH:]

    ab = jnp.dot(h, wab_ref[...], preferred_element_type=F32)
    abt = lax.dot_general(wabt_ref[...], h, NT_DIMS, preferred_element_type=F32)
    glog = -jnp.exp(alog_ref[...]) * _softplus(ab + dtb_ref[...])
    glogt = -jnp.exp(alogt_ref[...]) * _softplus(abt + dtbt_ref[...])
    r = lax.broadcasted_iota(jnp.int32, (B_IN_TILE, B_IN_TILE), 0)
    c = lax.broadcasted_iota(jnp.int32, (B_IN_TILE, B_IN_TILE), 1)
    shift = int(math.log2(B_CHUNK))
    same = jnp.right_shift(r, shift) == jnp.right_shift(c, shift)
    tril = jnp.where(same, jnp.where(r >= c, 1.0, 0.0), 0.0)
    triu = jnp.where(same, jnp.where(r <= c, 1.0, 0.0), 0.0)
    gc = _dot_hi(tril, glog)
    gct = _dot_hi(glogt, triu)
    is_decay = lax.broadcasted_iota(jnp.int32, (1, 2 * B_HEADS), 1) < B_HEADS
    gates_ref[...] = jnp.where(is_decay, gc, jax.nn.sigmoid(ab))
    for j in range(B_IN_TILE // B_CHUNK):
        gct_ref[j] = gct[:B_HEADS, j * B_CHUNK:(j + 1) * B_CHUNK]


def _b_in(x, g, wqkv, wz, wab, conv_w, a_log, dt_bias):
    t = x.shape[0]
    tm = B_IN_TILE
    blocks_prev = tm // B_PREV
    wabt = wab.T
    pad = jnp.zeros((1, B_HEADS), F32)
    a_log = jnp.concatenate([a_log.reshape(1, B_HEADS).astype(F32), pad], axis=1)
    dt_bias = jnp.concatenate([dt_bias.reshape(1, B_HEADS).astype(F32), pad], axis=1)
    return pl.pallas_call(
        _b_in_kernel, grid=(t // tm,),
        in_specs=[pl.BlockSpec((tm, D_MODEL), lambda i: (i, 0)),
                  pl.BlockSpec((B_PREV, D_MODEL), lambda i: (jnp.maximum(i * blocks_prev - 1, 0), 0)),
                  _const_spec((1, D_MODEL)), _const_spec((D_MODEL, 3 * B_WIDTH)),
                  _const_spec((D_MODEL, B_WIDTH)), _const_spec((D_MODEL, 2 * B_HEADS)),
                  _const_spec((2 * B_HEADS, D_MODEL)), _const_spec((B_CONV, 3 * B_WIDTH)),
                  _const_spec((1, 2 * B_HEADS)), _const_spec((1, 2 * B_HEADS)),
                  _const_spec((2 * B_HEADS, 1)), _const_spec((2 * B_HEADS, 1))],
        out_specs=[pl.BlockSpec((tm, 3 * B_WIDTH), lambda i: (i, 0)),
                   pl.BlockSpec((tm, B_WIDTH), lambda i: (i, 0)),
                   pl.BlockSpec((tm, 2 * B_HEADS), lambda i: (i, 0)),
                   pl.BlockSpec((tm // B_CHUNK, B_HEADS, B_CHUNK), lambda i: (i, 0, 0))],
        out_shape=[jax.ShapeDtypeStruct((t, 3 * B_WIDTH), F32),
                   jax.ShapeDtypeStruct((t, B_WIDTH), F32),
                   jax.ShapeDtypeStruct((t, 2 * B_HEADS), F32),
                   jax.ShapeDtypeStruct((t // B_CHUNK, B_HEADS, B_CHUNK), F32)],
        scratch_shapes=[pltpu.VMEM((B_PREV + tm, 3 * B_WIDTH), F32)],
        compiler_params=_params(1), name="b_in")(
            x, x, g, wqkv, wz, wab, wabt, conv_w, a_log, dt_bias, a_log.T, dt_bias.T)


def _unit_lower_inverse(low):
    n = low.shape[0]
    eye = jnp.where(lax.broadcasted_iota(jnp.int32, (n, n), 0)
                    == lax.broadcasted_iota(jnp.int32, (n, n), 1), 1.0, 0.0)
    x = eye - low
    p = low
    for _ in range(int(math.log2(n)) - 1):
        p = _dot_hi(p, p)
        x = x + _dot_hi(x, p)
    return x


def _b_prep_kernel(q_ref, k_ref, v_ref, gates_ref, gct_ref,
                   u_ref, w_ref, qd_ref, kt_ref, attn_ref):
    ii = lax.broadcasted_iota(jnp.int32, (B_CHUNK, B_CHUNK), 0)
    jj = lax.broadcasted_iota(jnp.int32, (B_CHUNK, B_CHUNK), 1)
    attn_pair = []
    for h in range(B_HEADS):
        sl = slice(h * B_HEAD_DIM, (h + 1) * B_HEAD_DIM)
        q, k, v = q_ref[:, sl], k_ref[:, sl], v_ref[:, sl]
        gc = gates_ref[:, h:h + 1]
        beta = gates_ref[:, B_HEADS + h:B_HEADS + h + 1]
        g_last = gates_ref[B_CHUNK - 1:B_CHUNK, h:h + 1]
        egc = jnp.exp(gc)
        kb = k * beta
        kbf = k.astype(BF16)
        kk = lax.dot_general(kb.astype(BF16), kbf, NT_DIMS, preferred_element_type=F32)
        qk = lax.dot_general(q.astype(BF16), kbf, NT_DIMS, preferred_element_type=F32)
        decay = jnp.where(ii >= jj, jnp.exp(jnp.minimum(gc - gct_ref[0, h:h + 1, :], 0.0)), 0.0)
        tinv = _unit_lower_inverse(jnp.where(ii > jj, kk * decay, 0.0)).astype(BF16)
        u_ref[:, sl] = jnp.dot(tinv, (v * beta).astype(BF16),
                               preferred_element_type=F32).astype(u_ref.dtype)
        w_ref[:, sl] = jnp.dot(tinv, (kb * egc).astype(BF16),
                               preferred_element_type=F32).astype(w_ref.dtype)
        qd_ref[:, sl] = (q * egc).astype(qd_ref.dtype)
        kt_ref[:, sl] = (k * jnp.exp(g_last - gc)).astype(kt_ref.dtype)
        attn_pair.append(qk * decay)
        if h % 2 == 1:
            attn_ref[:, (h - 1) * B_CHUNK:(h + 1) * B_CHUNK] = jnp.concatenate(
                attn_pair, axis=-1).astype(attn_ref.dtype)
            attn_pair = []


def _b_prep(qkv, gates, gct):
    t = qkv.shape[0]
    n = t // B_CHUNK
    wide = lambda col: pl.BlockSpec((B_CHUNK, B_WIDTH), lambda i, col=col: (i, col))
    out_wide = jax.ShapeDtypeStruct((t, B_WIDTH), BF16)
    return pl.pallas_call(
        _b_prep_kernel, grid=(n,),
        in_specs=[wide(0), wide(1), wide(2),
                  pl.BlockSpec((B_CHUNK, 2 * B_HEADS), lambda i: (i, 0)),
                  pl.BlockSpec((1, B_HEADS, B_CHUNK), lambda i: (i, 0, 0))],
        out_specs=[wide(0)] * 4 + [pl.BlockSpec((B_CHUNK, B_HEADS * B_CHUNK), lambda i: (i, 0))],
        out_shape=[out_wide] * 4 + [jax.ShapeDtypeStruct((t, B_HEADS * B_CHUNK), BF16)],
        compiler_params=_params(1), name="b_prep")(qkv, qkv, qkv, gates, gct)


def _b_scan_kernel(u_ref, w_ref, qd_ref, kt_ref, attn_ref, gates_ref, z_ref, nw_ref,
                   y_ref, state):
    @pl.when(pl.program_id(1) == 0)
    def _():
        state[...] = jnp.zeros_like(state)

    nw = nw_ref[...]
    zero = jnp.zeros((B_CHUNK, B_HEAD_DIM), BF16)
    for hp in range(B_HEADS // 2):
        v_new = []
        q_s = []
        for h in (2 * hp, 2 * hp + 1):
            sl = slice(h * B_HEAD_DIM, (h + 1) * B_HEAD_DIM)
            s_bf = state[h].astype(BF16)
            wq = jnp.concatenate([w_ref[:, sl], qd_ref[:, sl]], axis=0)
            r = jnp.dot(wq, s_bf, preferred_element_type=F32)
            vn = (u_ref[:, sl].astype(F32) - r[:B_CHUNK]).astype(BF16)
            v_new.append(vn)
            q_s.append(r[B_CHUNK:])
            chunk_decay = jnp.exp(gates_ref[B_CHUNK - 1:B_CHUNK, h:h + 1])
            state[h] = state[h] * chunk_decay + lax.dot_general(
                kt_ref[:, sl], vn, TN_DIMS, preferred_element_type=F32)
        vblock = jnp.concatenate(
            [jnp.concatenate([v_new[0], zero], axis=1), jnp.concatenate([zero, v_new[1]], axis=1)],
            axis=0)
        intra = jnp.dot(attn_ref[:, 2 * hp * B_CHUNK:(2 * hp + 2) * B_CHUNK], vblock,
                        preferred_element_type=F32)
        for j, h in enumerate((2 * hp, 2 * hp + 1)):
            sl = slice(h * B_HEAD_DIM, (h + 1) * B_HEAD_DIM)
            o = q_s[j] + intra[:, j * B_HEAD_DIM:(j + 1) * B_HEAD_DIM]
            y = _rms(o, nw) * _silu(z_ref[:, sl])
            y_ref[:, sl] = y.astype(y_ref.dtype)


def _b_scan(u, w, qd, kt, attn, gates, z, norm_w):
    t = u.shape[0]
    b = t // SEQ
    idx = lambda bi, n: (bi * N_CHUNKS + n, 0)
    wide = pl.BlockSpec((B_CHUNK, B_WIDTH), idx)
    return pl.pallas_call(
        _b_scan_kernel, grid=(b, N_CHUNKS),
        in_specs=[wide, wide, wide, wide,
                  pl.BlockSpec((B_CHUNK, B_HEADS * B_CHUNK), idx),
                  pl.BlockSpec((B_CHUNK, 2 * B_HEADS), idx),
                  wide, _const_spec((1, B_HEAD_DIM))],
        out_specs=wide,
        out_shape=jax.ShapeDtypeStruct((t, B_WIDTH), BF16),
        scratch_shapes=[pltpu.VMEM((B_HEADS, B_HEAD_DIM, B_HEAD_DIM), F32)],
        compiler_params=pltpu.CompilerParams(dimension_semantics=("parallel", "arbitrary"),
                                             vmem_limit_bytes=VMEM_LIMIT),
        name="b_scan")(u, w, qd, kt, attn, gates, z, norm_w)


def kernel(x, norm_g, ffn_w_gate, ffn_w_up, ffn_w_down, rel_bias, a_w_in, a_w_out,
           b_w_in, b_conv_w, b_a_log, b_dt_bias, b_norm_w, b_w_out, final_g):
    batch, seq, d = x.shape
    assert (seq, d) == (SEQ, D_MODEL)
    depth = norm_g.shape[0]
    t = batch * seq
    x = x.reshape(t, d)
    bias = _a_bias_tables(rel_bias)
    norm_g = norm_g.astype(F32)
    for i in range(depth):
        j = i // 2
        x = _ffn(x, norm_g[i, 0][None], ffn_w_gate[i, 0].astype(BF16), ffn_w_up[i, 0].astype(BF16),
                 ffn_w_down[i, 0].astype(BF16))
        g_mix = norm_g[i, 1][None]
        if i % 2 == 0:
            proj = _a_in(x, g_mix, a_w_in[j].astype(BF16))
            o = _a_attn(proj.reshape(batch, seq, A_IN_WIDTH), bias)
            x = _proj_res(x, o.reshape(t, A_GROUP_WIDTH), a_w_out[j].astype(BF16))
        else:
            w_in = b_w_in[j].astype(BF16)
            qkv, z, gates, gct = _b_in(x, g_mix, w_in[:, :3 * B_WIDTH], w_in[:, 3 * B_WIDTH:4 * B_WIDTH],
                                       w_in[:, 4 * B_WIDTH:], b_conv_w[j].astype(F32),
                                       b_a_log[j], b_dt_bias[j])
            u, w, qd, kt, attn = _b_prep(qkv, gates, gct)
            y = _b_scan(u, w, qd, kt, attn, gates, z, b_norm_w[j].astype(F32)[None])
            x = _proj_res(x, y, b_w_out[j].astype(BF16))
        last = i == depth - 1
        x = _ffn(x, norm_g[i, 2][None], ffn_w_gate[i, 1].astype(BF16), ffn_w_up[i, 1].astype(BF16),
                 ffn_w_down[i, 1].astype(BF16), final_g=final_g.astype(F32)[None] if last else None)
    return x.reshape(batch, seq, d)
```

```python
import functools
import math

import numpy as np
import jax
import jax.numpy as jnp
from jax import lax
from jax.experimental import pallas as pl
from jax.experimental.pallas import tpu as pltpu

F32 = jnp.float32
BF16 = jnp.bfloat16

D_MODEL = 1024
SEQ = 2048
D_FF = 2816
NORM_EPS = 1e-6
MACARON_WEIGHT = 0.5

A_PATTERNS = ((128, 1), (512, 4), (2048, 16))
A_GROUPS = 3
A_HEADS = 8
A_HEAD_DIM = 64
A_GROUP_WIDTH = A_HEADS * A_HEAD_DIM
A_IN_WIDTH = A_GROUPS * 3 * A_GROUP_WIDTH
A_BLOCK = 128
A_PAIR = 2 * A_HEAD_DIM
NEG_INF = -1e30
NUM_BUCKETS = 32
MAX_DISTANCE = 2048

B_HEADS = 8
B_HEAD_DIM = 128
B_WIDTH = B_HEADS * B_HEAD_DIM
B_CONV = 4
B_CHUNK = 64
N_CHUNKS = SEQ // B_CHUNK

VMEM_LIMIT = 56 * 1024 * 1024

NT_DIMS = (((1,), (1,)), ((), ()))
TN_DIMS = (((0,), (0,)), ((), ()))


def _const_spec(shape):
    zeros = (0,) * len(shape)
    return pl.BlockSpec(shape, lambda *_: zeros, pipeline_mode=pl.Buffered(1))


def _params(n_axes):
    return pltpu.CompilerParams(dimension_semantics=("parallel",) * n_axes,
                                vmem_limit_bytes=VMEM_LIMIT)


def _rms(x, g):
    ms = jnp.mean(x * x, axis=-1, keepdims=True)
    return x * lax.rsqrt(ms + NORM_EPS) * g


def _silu(x):
    return x * jax.nn.sigmoid(x)


def _softplus(x):
    return jnp.maximum(x, 0.0) + jnp.log1p(jnp.exp(-jnp.abs(x)))


def _dot_hi(a, b):
    return jnp.dot(a, b, preferred_element_type=F32, precision=lax.Precision.HIGHEST)


def _ffn_kernel(x_ref, g_ref, wg_ref, wu_ref, wd_ref, *rest, final):
    o_ref = rest[-1]
    x = x_ref[...]
    h = _rms(x, g_ref[...]).astype(BF16)
    gate = jnp.dot(h, wg_ref[...], preferred_element_type=F32)
    up = jnp.dot(h, wu_ref[...], preferred_element_type=F32)
    a = (_silu(gate) * up).astype(BF16)
    y = x + MACARON_WEIGHT * jnp.dot(a, wd_ref[...], preferred_element_type=F32)
    if final:
        y = _rms(y, rest[0][...])
    o_ref[...] = y


def _ffn(x, g, wg, wu, wd, final_g=None, tm=512):
    t = x.shape[0]
    row = pl.BlockSpec((tm, D_MODEL), lambda i: (i, 0))
    in_specs = [row, _const_spec((1, D_MODEL)), _const_spec((D_MODEL, D_FF)),
                _const_spec((D_MODEL, D_FF)), _const_spec((D_FF, D_MODEL))]
    args = [x, g, wg, wu, wd]
    if final_g is not None:
        in_specs.append(_const_spec((1, D_MODEL)))
        args.append(final_g)
    return pl.pallas_call(
        functools.partial(_ffn_kernel, final=final_g is not None),
        grid=(t // tm,), in_specs=in_specs, out_specs=row,
        out_shape=jax.ShapeDtypeStruct((t, D_MODEL), F32),
        compiler_params=_params(1), name="ffn")(*args)


def _proj_res_kernel(x_ref, y_ref, w_ref, o_ref):
    o_ref[...] = x_ref[...] + jnp.dot(y_ref[...], w_ref[...], preferred_element_type=F32)


def _proj_res(x, y, w, tm=1024):
    t, k = y.shape
    return pl.pallas_call(
        _proj_res_kernel, grid=(t // tm,),
        in_specs=[pl.BlockSpec((tm, D_MODEL), lambda i: (i, 0)),
                  pl.BlockSpec((tm, k), lambda i: (i, 0)),
                  _const_spec((k, D_MODEL))],
        out_specs=pl.BlockSpec((tm, D_MODEL), lambda i: (i, 0)),
        out_shape=jax.ShapeDtypeStruct((t, D_MODEL), F32),
        compiler_params=_params(1), name="proj_res")(x, y, w)


def _a_in_kernel(x_ref, g_ref, w_ref, o_ref):
    h = _rms(x_ref[...], g_ref[...]).astype(BF16)
    o_ref[...] = jnp.dot(h, w_ref[...], preferred_element_type=F32)


def _a_in(x, g, w, tm=512):
    t = x.shape[0]
    return pl.pallas_call(
        _a_in_kernel, grid=(t // tm,),
        in_specs=[pl.BlockSpec((tm, D_MODEL), lambda i: (i, 0)),
                  _const_spec((1, D_MODEL)), _const_spec((D_MODEL, A_IN_WIDTH))],
        out_specs=pl.BlockSpec((tm, A_IN_WIDTH), lambda i: (i, 0)),
        out_shape=jax.ShapeDtypeStruct((t, A_IN_WIDTH), F32),
        compiler_params=_params(1), name="a_in")(x, g, w)


def _t5_bucket_np(distance):
    max_exact = NUM_BUCKETS // 2
    n = distance.astype(np.float32)
    large = np.float32(max_exact) + (
        np.log(np.maximum(n, np.float32(1.0)) / np.float32(max_exact))
        / np.float32(math.log(MAX_DISTANCE / max_exact)) * np.float32(NUM_BUCKETS - max_exact))
    large = np.minimum(large.astype(np.int32), NUM_BUCKETS - 1)
    return np.where(distance < max_exact, distance, large).astype(np.int32)


def _a_bias_tables(rel_bias):
    q_loc = np.arange(A_BLOCK)[:, None]
    k_loc = np.arange(2 * A_BLOCK)[None, :]
    rel = q_loc + A_BLOCK - k_loc
    tabs = []
    for g, (window, dilation) in enumerate(A_PATTERNS):
        steps = window // dilation
        bucket = _t5_bucket_np(np.maximum(rel, 0) * dilation)
        valid = (rel >= 0) & (rel <= steps)
        b = rel_bias.astype(F32)[:, g * A_HEADS:(g + 1) * A_HEADS][bucket]
        tabs.append(jnp.where(valid[..., None], b, NEG_INF).transpose(2, 0, 1))
    return jnp.stack(tabs)


def _a_unit(q_ref, k_ref, v_ref, bias_ref, g, o_scr, l_scr, row0, dilation, first):
    def rows(start, n):
        if dilation == 1:
            return pl.ds(start, n)
        return pl.ds(start, n, stride=dilation)

    nk = A_BLOCK if first else 2 * A_BLOCK
    k0 = row0 if first else row0 - A_BLOCK * dilation
    q = q_ref[rows(row0, A_BLOCK), :] * (A_HEAD_DIM ** -0.5)
    k = k_ref[rows(k0, nk), :].astype(BF16)
    v = v_ref[rows(k0, nk), :]
    head0 = lax.broadcasted_iota(jnp.int32, (1, A_PAIR), 1) < A_HEAD_DIM
    res, mx = [], []
    for h in range(2):
        mine = head0 if h == 0 else jnp.logical_not(head0)
        qh = jnp.where(mine, q, 0.0).astype(BF16)
        s = lax.dot_general(qh, k, NT_DIMS, preferred_element_type=F32)
        if first:
            s = s + bias_ref[g, h, :, A_BLOCK:]
        else:
            s = s + bias_ref[g, h]
        m = jnp.max(s, axis=-1, keepdims=True)
        p = jnp.exp(s - m).astype(BF16)
        ve = jnp.where(mine, v, 1.0).astype(BF16)
        res.append(jnp.dot(p, ve, preferred_element_type=F32))
        mx.append(m)
    acc = jnp.where(head0, res[0], res[1])
    den = pltpu.roll(jnp.where(head0, res[1], res[0]), A_HEAD_DIM, 1)
    lse = jnp.where(head0, mx[0], mx[1]) + jnp.log(den)
    o_scr[g, rows(row0, A_BLOCK), :] = acc / den
    l_scr[g, rows(row0, A_BLOCK), :] = lse


def _a_attn_kernel(q0, k0, v0, q1, k1, v1, q2, k2, v2, bias_ref, o_ref, o_scr, l_scr):
    qkv = ((q0, k0, v0), (q1, k1, v1), (q2, k2, v2))
    for g, (_, dilation) in enumerate(A_PATTERNS):
        q_ref, k_ref, v_ref = qkv[g]
        nb = SEQ // dilation // A_BLOCK
        unit = functools.partial(_a_unit, q_ref, k_ref, v_ref, bias_ref, g, o_scr, l_scr,
                                 dilation=dilation)

        def per_subseq(c, carry, unit=unit, nb=nb, dilation=dilation):
            unit(row0=c, first=True)

            def per_block(i, carry2):
                unit(row0=c + i * (A_BLOCK * dilation), first=False)
                return carry2

            if nb > 1:
                lax.fori_loop(1, nb, per_block, 0)
            return carry

        if dilation == 1:
            per_subseq(0, 0)
        else:
            lax.fori_loop(0, dilation, per_subseq, 0)

    rows_per_step = 256

    def merge(r, carry):
        sl = pl.ds(pl.multiple_of(r * rows_per_step, rows_per_step), rows_per_step)
        l0, l1, l2 = l_scr[0, sl, :], l_scr[1, sl, :], l_scr[2, sl, :]
        m = jnp.maximum(jnp.maximum(l0, l1), l2)
        w0, w1, w2 = jnp.exp(l0 - m), jnp.exp(l1 - m), jnp.exp(l2 - m)
        num = w0 * o_scr[0, sl, :] + w1 * o_scr[1, sl, :] + w2 * o_scr[2, sl, :]
        o_ref[sl, :] = (num / (w0 + w1 + w2)).astype(o_ref.dtype)
        return carry

    lax.fori_loop(0, SEQ // rows_per_step, merge, 0)


def _a_attn(proj, bias):
    b = proj.shape[0]
    n_pairs = A_GROUP_WIDTH // A_PAIR
    in_specs = []
    for g in range(A_GROUPS):
        for r in range(3):
            col = (g * 3 + r) * n_pairs
            in_specs.append(pl.BlockSpec((None, SEQ, A_PAIR),
                                         lambda bi, hp, col=col: (bi, 0, col + hp)))
    in_specs.append(pl.BlockSpec((A_GROUPS, 2, A_BLOCK, 2 * A_BLOCK), lambda bi, hp: (0, hp, 0, 0)))
    return pl.pallas_call(
        _a_attn_kernel, grid=(b, n_pairs), in_specs=in_specs,
        out_specs=pl.BlockSpec((None, SEQ, A_PAIR), lambda bi, hp: (bi, 0, hp)),
        out_shape=jax.ShapeDtypeStruct((b, SEQ, A_GROUP_WIDTH), BF16),
        scratch_shapes=[pltpu.VMEM((A_GROUPS, SEQ, A_PAIR), F32),
                        pltpu.VMEM((A_GROUPS, SEQ, A_PAIR), F32)],
        compiler_params=_params(2), name="a_attn")(*([proj] * 9), bias)


B_IN_TILE = 512
B_PREV = 8


def _b_in_kernel(x_ref, xp_ref, g_ref, wqkv_ref, wz_ref, wab_ref, wabt_ref, conv_ref,
                 alog_ref, dtb_ref, alogt_ref, dtbt_ref,
                 qkv_ref, z_ref, gates_ref, gct_ref, pre_scr):
    tiles_per_seq = SEQ // B_IN_TILE
    i = pl.program_id(0)
    g = g_ref[...]
    h = _rms(x_ref[...], g).astype(BF16)
    hp = _rms(xp_ref[...], g).astype(BF16)
    keep = jnp.where(i % tiles_per_seq == 0, 0.0, 1.0)
    pre_scr[pl.ds(0, B_PREV), :] = keep * jnp.dot(hp, wqkv_ref[...], preferred_element_type=F32)
    pre_scr[pl.ds(B_PREV, B_IN_TILE), :] = jnp.dot(h, wqkv_ref[...], preferred_element_type=F32)
    z_ref[...] = jnp.dot(h, wz_ref[...], preferred_element_type=F32)

    conv = conv_ref[B_CONV - 1:B_CONV, :] * pre_scr[pl.ds(B_PREV, B_IN_TILE), :]
    for j in range(B_CONV - 1):
        conv = conv + conv_ref[j:j + 1, :] * pre_scr[pl.ds(B_PREV - (B_CONV - 1) + j, B_IN_TILE), :]
    act = _silu(conv)
    for hd in range(2 * B_HEADS):
        sl = slice(hd * B_HEAD_DIM, (hd + 1) * B_HEAD_DIM)
        t = act[:, sl]
        n = t * lax.rsqrt(jnp.sum(t * t, axis=-1, keepdims=True) + NORM_EPS)
        if hd < B_HEADS:
            n = n * (B_HEAD_DIM ** -0.5)
        qkv_ref[:, sl] = n
    qkv_ref[:, 2 * B_WIDTH:] = act[:, 2 * B_WIDTH:]

    ab = jnp.dot(h, wab_ref[...], preferred_element_type=F32)
    abt = lax.dot_general(wabt_ref[...], h, NT_DIMS, preferred_element_type=F32)
    glog = -jnp.exp(alog_ref[...]) * _softplus(ab + dtb_ref[...])
    glogt = -jnp.exp(alogt_ref[...]) * _softplus(abt + dtbt_ref[...])
    r = lax.broadcasted_iota(jnp.int32, (B_IN_TILE, B_IN_TILE), 0)
    c = lax.broadcasted_iota(jnp.int32, (B_IN_TILE, B_IN_TILE), 1)
    shift = int(math.log2(B_CHUNK))
    same = jnp.right_shift(r, shift) == jnp.right_shift(c, shift)
    tril = jnp.where(same, jnp.where(r >= c, 1.0, 0.0), 0.0)
    triu = jnp.where(same, jnp.where(r <= c, 1.0, 0.0), 0.0)
    gc = _dot_hi(tril, glog)
    gct = _dot_hi(glogt, triu)
    is_decay = lax.broadcasted_iota(jnp.int32, (1, 2 * B_HEADS), 1) < B_HEADS
    gates_ref[...] = jnp.where(is_decay, gc, jax.nn.sigmoid(ab))
    for j in range(B_IN_TILE // B_CHUNK):
        gct_ref[j] = gct[:B_HEADS, j * B_CHUNK:(j + 1) * B_CHUNK]


def _b_in(x, g, wqkv, wz, wab, conv_w, a_log, dt_bias):
    t = x.shape[0]
    tm = B_IN_TILE
    blocks_prev = tm // B_PREV
    wabt = wab.T
    pad = jnp.zeros((1, B_HEADS), F32)
    a_log = jnp.concatenate([a_log.reshape(1, B_HEADS).astype(F32), pad], axis=1)
    dt_bias = jnp.concatenate([dt_bias.reshape(1, B_HEADS).astype(F32), pad], axis=1)
    return pl.pallas_call(
        _b_in_kernel, grid=(t // tm,),
        in_specs=[pl.BlockSpec((tm, D_MODEL), lambda i: (i, 0)),
                  pl.BlockSpec((B_PREV, D_MODEL), lambda i: (jnp.maximum(i * blocks_prev - 1, 0), 0)),
                  _const_spec((1, D_MODEL)), _const_spec((D_MODEL, 3 * B_WIDTH)),
                  _const_spec((D_MODEL, B_WIDTH)), _const_spec((D_MODEL, 2 * B_HEADS)),
                  _const_spec((2 * B_HEADS, D_MODEL)), _const_spec((B_CONV, 3 * B_WIDTH)),
                  _const_spec((1, 2 * B_HEADS)), _const_spec((1, 2 * B_HEADS)),
                  _const_spec((2 * B_HEADS, 1)), _const_spec((2 * B_HEADS, 1))],
        out_specs=[pl.BlockSpec((tm, 3 * B_WIDTH), lambda i: (i, 0)),
                   pl.BlockSpec((tm, B_WIDTH), lambda i: (i, 0)),
                   pl.BlockSpec((tm, 2 * B_HEADS), lambda i: (i, 0)),
                   pl.BlockSpec((tm // B_CHUNK, B_HEADS, B_CHUNK), lambda i: (i, 0, 0))],
        out_shape=[jax.ShapeDtypeStruct((t, 3 * B_WIDTH), F32),
                   jax.ShapeDtypeStruct((t, B_WIDTH), F32),
                   jax.ShapeDtypeStruct((t, 2 * B_HEADS), F32),
                   jax.ShapeDtypeStruct((t // B_CHUNK, B_HEADS, B_CHUNK), F32)],
        scratch_shapes=[pltpu.VMEM((B_PREV + tm, 3 * B_WIDTH), F32)],
        compiler_params=_params(1), name="b_in")(
            x, x, g, wqkv, wz, wab, wabt, conv_w, a_log, dt_bias, a_log.T, dt_bias.T)


B_STEP_CHUNKS = 4
B_PAIRS = B_HEADS // 2
B_PAIR_W = 2 * B_CHUNK
INV_ROWS = 4


def _chunk_decay(gates_ref, gct_ref, c, rows, h, keep):
    gc = gates_ref[rows, h:h + 1]
    return jnp.where(keep, jnp.exp(jnp.minimum(gc - gct_ref[c, h:h + 1, :], 0.0)), 0.0)


def _b_low_kernel(k_ref, gates_ref, gct_ref, l_ref):
    strict = (lax.broadcasted_iota(jnp.int32, (B_CHUNK, B_CHUNK), 0)
              > lax.broadcasted_iota(jnp.int32, (B_CHUNK, B_CHUNK), 1))

    def chunk(c, carry):
        rows = pl.ds(pl.multiple_of(c * B_CHUNK, B_CHUNK), B_CHUNK)
        pair = []
        for h in range(B_HEADS):
            k = k_ref[rows, h * B_HEAD_DIM:(h + 1) * B_HEAD_DIM]
            beta = gates_ref[rows, B_HEADS + h:B_HEADS + h + 1]
            kk = lax.dot_general((k * beta).astype(BF16), k.astype(BF16), NT_DIMS,
                                 preferred_element_type=F32)
            pair.append(kk * _chunk_decay(gates_ref, gct_ref, c, rows, h, strict))
            if h % 2 == 1:
                l_ref[h // 2, rows, :] = jnp.concatenate(pair, axis=-1)
                pair = []
        return carry

    lax.fori_loop(0, B_STEP_CHUNKS, chunk, 0)


def _b_low(qkv, gates, gct):
    t = qkv.shape[0]
    tm = B_STEP_CHUNKS * B_CHUNK
    return pl.pallas_call(
        _b_low_kernel, grid=(t // tm,),
        in_specs=[pl.BlockSpec((tm, B_WIDTH), lambda i: (i, 1)),
                  pl.BlockSpec((tm, 2 * B_HEADS), lambda i: (i, 0)),
                  pl.BlockSpec((B_STEP_CHUNKS, B_HEADS, B_CHUNK), lambda i: (i, 0, 0))],
        out_specs=pl.BlockSpec((B_PAIRS, tm, B_PAIR_W), lambda i: (0, i, 0)),
        out_shape=jax.ShapeDtypeStruct((B_PAIRS, t, B_PAIR_W), F32),
        compiler_params=_params(1), name="b_low")(qkv, gates, gct)


def _b_inv_kernel(l_ref, t_ref, lt_scr, xt_scr):
    def to_problem_lanes(i, carry):
        slab = jnp.concatenate(
            [l_ref[pr, pl.ds(i, N_CHUNKS, stride=B_CHUNK), :] for pr in range(B_PAIRS)], axis=0)
        slab_t = slab.T
        lt_scr[0, i] = slab_t[:B_CHUNK]
        lt_scr[1, i] = slab_t[B_CHUNK:]
        return carry

    lax.fori_loop(0, B_CHUNK, to_problem_lanes, 0)

    col = lax.broadcasted_iota(jnp.int32, (B_CHUNK, N_CHUNKS * B_PAIRS), 0)
    for half in range(2):
        def row_group(ib, carry, half=half):
            i0 = ib * INV_ROWS
            accs = tuple(jnp.where(col == i0 + r, 1.0, 0.0) for r in range(INV_ROWS))

            def earlier_row(j, accs):
                xj = xt_scr[half, j]
                return tuple(accs[r] - lt_scr[half, i0 + r, pl.ds(j, 1), :] * xj
                             for r in range(INV_ROWS))

            accs = list(lax.fori_loop(0, i0, earlier_row, accs))
            for r in range(INV_ROWS):
                for r2 in range(r):
                    accs[r] = accs[r] - lt_scr[half, i0 + r, pl.ds(i0 + r2, 1), :] * accs[r2]
                xt_scr[half, i0 + r] = accs[r]
            return carry

        lax.fori_loop(0, B_CHUNK // INV_ROWS, row_group, 0)

    def from_problem_lanes(i, carry):
        slab_t = jnp.concatenate([xt_scr[0, i], xt_scr[1, i]], axis=0).T
        for pr in range(B_PAIRS):
            t_ref[pr, pl.ds(i, N_CHUNKS, stride=B_CHUNK), :] = (
                slab_t[pr * N_CHUNKS:(pr + 1) * N_CHUNKS])
        return carry

    lax.fori_loop(0, B_CHUNK, from_problem_lanes, 0)


def _b_inv(low):
    t = low.shape[1]
    block = pl.BlockSpec((B_PAIRS, SEQ, B_PAIR_W), lambda b: (0, b, 0))
    scr = pltpu.VMEM((2, B_CHUNK, B_CHUNK, N_CHUNKS * B_PAIRS), F32)
    return pl.pallas_call(
        _b_inv_kernel, grid=(t // SEQ,), in_specs=[block], out_specs=block,
        out_shape=jax.ShapeDtypeStruct(low.shape, F32), scratch_shapes=[scr, scr],
        compiler_params=_params(1), name="b_inv")(low)


def _b_prep_kernel(q_ref, k_ref, v_ref, gates_ref, gct_ref, t_ref,
                   u_ref, w_ref, qd_ref, kt_ref, attn_ref):
    tril = (lax.broadcasted_iota(jnp.int32, (B_CHUNK, B_CHUNK), 0)
            >= lax.broadcasted_iota(jnp.int32, (B_CHUNK, B_CHUNK), 1))

    def chunk(c, carry):
        rows = pl.ds(pl.multiple_of(c * B_CHUNK, B_CHUNK), B_CHUNK)
        last = pl.ds(c * B_CHUNK + B_CHUNK - 1, 1)
        pair = []
        for h in range(B_HEADS):
            sl = slice(h * B_HEAD_DIM, (h + 1) * B_HEAD_DIM)
            q, k, v = q_ref[rows, sl], k_ref[rows, sl], v_ref[rows, sl]
            gc = gates_ref[rows, h:h + 1]
            beta = gates_ref[rows, B_HEADS + h:B_HEADS + h + 1]
            egc = jnp.exp(gc)
            kb = k * beta
            qk = lax.dot_general(q.astype(BF16), k.astype(BF16), NT_DIMS, preferred_element_type=F32)
            tinv = t_ref[h // 2, rows, (h % 2) * B_CHUNK:(h % 2 + 1) * B_CHUNK].astype(BF16)
            u_ref[rows, sl] = jnp.dot(tinv, (v * beta).astype(BF16),
                                      preferred_element_type=F32).astype(u_ref.dtype)
            w_ref[rows, sl] = jnp.dot(tinv, (kb * egc).astype(BF16),
                                      preferred_element_type=F32).astype(w_ref.dtype)
            qd_ref[rows, sl] = (q * egc).astype(qd_ref.dtype)
            kt_ref[rows, sl] = (k * jnp.exp(gates_ref[last, h:h + 1] - gc)).astype(kt_ref.dtype)
            pair.append(qk * _chunk_decay(gates_ref, gct_ref, c, rows, h, tril))
            if h % 2 == 1:
                attn_ref[rows, (h - 1) * B_CHUNK:(h + 1) * B_CHUNK] = jnp.concatenate(
                    pair, axis=-1).astype(attn_ref.dtype)
                pair = []
        return carry

    lax.fori_loop(0, B_STEP_CHUNKS, chunk, 0)


def _b_prep(qkv, gates, gct, tinv):
    t = qkv.shape[0]
    tm = B_STEP_CHUNKS * B_CHUNK
    wide = lambda col: pl.BlockSpec((tm, B_WIDTH), lambda i, col=col: (i, col))
    narrow = pl.BlockSpec((tm, B_HEADS * B_CHUNK), lambda i: (i, 0))
    out_wide = jax.ShapeDtypeStruct((t, B_WIDTH), BF16)
    return pl.pallas_call(
        _b_prep_kernel, grid=(t // tm,),
        in_specs=[wide(0), wide(1), wide(2),
                  pl.BlockSpec((tm, 2 * B_HEADS), lambda i: (i, 0)),
                  pl.BlockSpec((B_STEP_CHUNKS, B_HEADS, B_CHUNK), lambda i: (i, 0, 0)),
                  pl.BlockSpec((B_PAIRS, tm, B_PAIR_W), lambda i: (0, i, 0))],
        out_specs=[wide(0)] * 4 + [narrow],
        out_shape=[out_wide] * 4 + [jax.ShapeDtypeStruct((t, B_HEADS * B_CHUNK), BF16)],
        compiler_params=_params(1), name="b_prep")(qkv, qkv, qkv, gates, gct, tinv)


def _b_scan_kernel(u_ref, w_ref, qd_ref, kt_ref, attn_ref, gates_ref, z_ref, nw_ref,
                   y_ref, state):
    @pl.when(pl.program_id(1) == 0)
    def _():
        state[...] = jnp.zeros_like(state)

    nw = nw_ref[...]
    zero = jnp.zeros((B_CHUNK, B_HEAD_DIM), BF16)
    for hp in range(B_HEADS // 2):
        v_new = []
        q_s = []
        for h in (2 * hp, 2 * hp + 1):
            sl = slice(h * B_HEAD_DIM, (h + 1) * B_HEAD_DIM)
            s_bf = state[h].astype(BF16)
            wq = jnp.concatenate([w_ref[:, sl], qd_ref[:, sl]], axis=0)
            r = jnp.dot(wq, s_bf, preferred_element_type=F32)
            vn = (u_ref[:, sl].astype(F32) - r[:B_CHUNK]).astype(BF16)
            v_new.append(vn)
            q_s.append(r[B_CHUNK:])
            chunk_decay = jnp.exp(gates_ref[B_CHUNK - 1:B_CHUNK, h:h + 1])
            state[h] = state[h] * chunk_decay + lax.dot_general(
                kt_ref[:, sl], vn, TN_DIMS, preferred_element_type=F32)
        vblock = jnp.concatenate(
            [jnp.concatenate([v_new[0], zero], axis=1), jnp.concatenate([zero, v_new[1]], axis=1)],
            axis=0)
        intra = jnp.dot(attn_ref[:, 2 * hp * B_CHUNK:(2 * hp + 2) * B_CHUNK], vblock,
                        preferred_element_type=F32)
        for j, h in enumerate((2 * hp, 2 * hp + 1)):
            sl = slice(h * B_HEAD_DIM, (h + 1) * B_HEAD_DIM)
            o = q_s[j] + intra[:, j * B_HEAD_DIM:(j + 1) * B_HEAD_DIM]
            y = _rms(o, nw) * _silu(z_ref[:, sl])
            y_ref[:, sl] = y.astype(y_ref.dtype)


def _b_scan(u, w, qd, kt, attn, gates, z, norm_w):
    t = u.shape[0]
    b = t // SEQ
    idx = lambda bi, n: (bi * N_CHUNKS + n, 0)
    wide = pl.BlockSpec((B_CHUNK, B_WIDTH), idx)
    return pl.pallas_call(
        _b_scan_kernel, grid=(b, N_CHUNKS),
        in_specs=[wide, wide, wide, wide,
                  pl.BlockSpec((B_CHUNK, B_HEADS * B_CHUNK), idx),
                  pl.BlockSpec((B_CHUNK, 2 * B_HEADS), idx),
                  wide, _const_spec((1, B_HEAD_DIM))],
        out_specs=wide,
        out_shape=jax.ShapeDtypeStruct((t, B_WIDTH), BF16),
        scratch_shapes=[pltpu.VMEM((B_HEADS, B_HEAD_DIM, B_HEAD_DIM), F32)],
        compiler_params=pltpu.CompilerParams(dimension_semantics=("parallel", "arbitrary"),
                                             vmem_limit_bytes=VMEM_LIMIT),
        name="b_scan")(u, w, qd, kt, attn, gates, z, norm_w)


def kernel(x, norm_g, ffn_w_gate, ffn_w_up, ffn_w_down, rel_bias, a_w_in, a_w_out,
           b_w_in, b_conv_w, b_a_log, b_dt_bias, b_norm_w, b_w_out, final_g):
    batch, seq, d = x.shape
    assert (seq, d) == (SEQ, D_MODEL)
    depth = norm_g.shape[0]
    t = batch * seq
    x = x.reshape(t, d)
    bias = _a_bias_tables(rel_bias)
    norm_g = norm_g.astype(F32)
    for i in range(depth):
        j = i // 2
        x = _ffn(x, norm_g[i, 0][None], ffn_w_gate[i, 0].astype(BF16), ffn_w_up[i, 0].astype(BF16),
                 ffn_w_down[i, 0].astype(BF16))
        g_mix = norm_g[i, 1][None]
        if i % 2 == 0:
            proj = _a_in(x, g_mix, a_w_in[j].astype(BF16))
            o = _a_attn(proj.reshape(batch, seq, A_IN_WIDTH), bias)
            x = _proj_res(x, o.reshape(t, A_GROUP_WIDTH), a_w_out[j].astype(BF16))
        else:
            w_in = b_w_in[j].astype(BF16)
            qkv, z, gates, gct = _b_in(x, g_mix, w_in[:, :3 * B_WIDTH], w_in[:, 3 * B_WIDTH:4 * B_WIDTH],
                                       w_in[:, 4 * B_WIDTH:], b_conv_w[j].astype(F32),
                                       b_a_log[j], b_dt_bias[j])
            tinv = _b_inv(_b_low(qkv, gates, gct))
            u, w, qd, kt, attn = _b_prep(qkv, gates, gct, tinv)
            y = _b_scan(u, w, qd, kt, attn, gates, z, b_norm_w[j].astype(F32)[None])
            x = _proj_res(x, y, b_w_out[j].astype(BF16))
        last = i == depth - 1
        x = _ffn(x, norm_g[i, 2][None], ffn_w_gate[i, 1].astype(BF16), ffn_w_up[i, 1].astype(BF16),
                 ffn_w_down[i, 1].astype(BF16), final_g=final_g.astype(F32)[None] if last else None)
    return x.reshape(batch, seq, d)
```

```python
import functools
import math

import numpy as np
import jax
import jax.numpy as jnp
from jax import lax
from jax.experimental import pallas as pl
from jax.experimental.pallas import tpu as pltpu

F32 = jnp.float32
BF16 = jnp.bfloat16

D_MODEL = 1024
SEQ = 2048
D_FF = 2816
NORM_EPS = 1e-6
MACARON_WEIGHT = 0.5

A_PATTERNS = ((128, 1), (512, 4), (2048, 16))
A_GROUPS = 3
A_HEADS = 8
A_HEAD_DIM = 64
A_GROUP_WIDTH = A_HEADS * A_HEAD_DIM
A_IN_WIDTH = A_GROUPS * 3 * A_GROUP_WIDTH
A_BLOCK = 128
A_PAIR = 2 * A_HEAD_DIM
A_UNROLL = 4
NEG_INF = -1e30
NUM_BUCKETS = 32
MAX_DISTANCE = 2048

B_HEADS = 8
B_HEAD_DIM = 128
B_WIDTH = B_HEADS * B_HEAD_DIM
B_CONV = 4
B_CHUNK = 64
N_CHUNKS = SEQ // B_CHUNK

VMEM_LIMIT = 56 * 1024 * 1024

NT_DIMS = (((1,), (1,)), ((), ()))
TN_DIMS = (((0,), (0,)), ((), ()))


def _const_spec(shape):
    zeros = (0,) * len(shape)
    return pl.BlockSpec(shape, lambda *_: zeros, pipeline_mode=pl.Buffered(1))


def _params(n_axes):
    return pltpu.CompilerParams(dimension_semantics=("parallel",) * n_axes,
                                vmem_limit_bytes=VMEM_LIMIT)


def _rms(x, g):
    ms = jnp.mean(x * x, axis=-1, keepdims=True)
    return x * lax.rsqrt(ms + NORM_EPS) * g


def _silu(x):
    return x * jax.nn.sigmoid(x)


def _softplus(x):
    return jnp.maximum(x, 0.0) + jnp.log1p(jnp.exp(-jnp.abs(x)))


def _dot_hi(a, b):
    return jnp.dot(a, b, preferred_element_type=F32, precision=lax.Precision.HIGHEST)


def _ffn_kernel(x_ref, g_ref, wg_ref, wu_ref, wd_ref, *rest, final):
    o_ref = rest[-1]
    x = x_ref[...]
    h = _rms(x, g_ref[...]).astype(BF16)
    gate = jnp.dot(h, wg_ref[...], preferred_element_type=F32)
    up = jnp.dot(h, wu_ref[...], preferred_element_type=F32)
    a = (_silu(gate) * up).astype(BF16)
    y = x + MACARON_WEIGHT * jnp.dot(a, wd_ref[...], preferred_element_type=F32)
    if final:
        y = _rms(y, rest[0][...])
    o_ref[...] = y


def _ffn(x, g, wg, wu, wd, final_g=None, tm=512):
    t = x.shape[0]
    row = pl.BlockSpec((tm, D_MODEL), lambda i: (i, 0))
    in_specs = [row, _const_spec((1, D_MODEL)), _const_spec((D_MODEL, D_FF)),
                _const_spec((D_MODEL, D_FF)), _const_spec((D_FF, D_MODEL))]
    args = [x, g, wg, wu, wd]
    if final_g is not None:
        in_specs.append(_const_spec((1, D_MODEL)))
        args.append(final_g)
    return pl.pallas_call(
        functools.partial(_ffn_kernel, final=final_g is not None),
        grid=(t // tm,), in_specs=in_specs, out_specs=row,
        out_shape=jax.ShapeDtypeStruct((t, D_MODEL), F32),
        compiler_params=_params(1), name="ffn")(*args)


def _proj_res_kernel(x_ref, y_ref, w_ref, o_ref):
    o_ref[...] = x_ref[...] + jnp.dot(y_ref[...], w_ref[...], preferred_element_type=F32)


def _proj_res(x, y, w, tm=1024):
    t, k = y.shape
    return pl.pallas_call(
        _proj_res_kernel, grid=(t // tm,),
        in_specs=[pl.BlockSpec((tm, D_MODEL), lambda i: (i, 0)),
                  pl.BlockSpec((tm, k), lambda i: (i, 0)),
                  _const_spec((k, D_MODEL))],
        out_specs=pl.BlockSpec((tm, D_MODEL), lambda i: (i, 0)),
        out_shape=jax.ShapeDtypeStruct((t, D_MODEL), F32),
        compiler_params=_params(1), name="proj_res")(x, y, w)


def _a_in_kernel(x_ref, g_ref, w_ref, o_ref):
    h = _rms(x_ref[...], g_ref[...]).astype(BF16)
    o_ref[...] = jnp.dot(h, w_ref[...], preferred_element_type=F32)


def _a_in(x, g, w, tm=512):
    t = x.shape[0]
    return pl.pallas_call(
        _a_in_kernel, grid=(t // tm,),
        in_specs=[pl.BlockSpec((tm, D_MODEL), lambda i: (i, 0)),
                  _const_spec((1, D_MODEL)), _const_spec((D_MODEL, A_IN_WIDTH))],
        out_specs=pl.BlockSpec((tm, A_IN_WIDTH), lambda i: (i, 0)),
        out_shape=jax.ShapeDtypeStruct((t, A_IN_WIDTH), F32),
        compiler_params=_params(1), name="a_in")(x, g, w)


def _t5_bucket_np(distance):
    max_exact = NUM_BUCKETS // 2
    n = distance.astype(np.float32)
    large = np.float32(max_exact) + (
        np.log(np.maximum(n, np.float32(1.0)) / np.float32(max_exact))
        / np.float32(math.log(MAX_DISTANCE / max_exact)) * np.float32(NUM_BUCKETS - max_exact))
    large = np.minimum(large.astype(np.int32), NUM_BUCKETS - 1)
    return np.where(distance < max_exact, distance, large).astype(np.int32)


def _a_bias_tables(rel_bias):
    q_loc = np.arange(A_BLOCK)[:, None]
    k_loc = np.arange(2 * A_BLOCK)[None, :]
    rel = q_loc + A_BLOCK - k_loc
    tabs = []
    for g, (window, dilation) in enumerate(A_PATTERNS):
        steps = window // dilation
        bucket = _t5_bucket_np(np.maximum(rel, 0) * dilation)
        valid = (rel >= 0) & (rel <= steps)
        b = rel_bias.astype(F32)[:, g * A_HEADS:(g + 1) * A_HEADS][bucket]
        tabs.append(jnp.where(valid[..., None], b, NEG_INF).transpose(2, 0, 1))
    return jnp.stack(tabs)


def _a_unit(q_ref, k_ref, v_ref, bias_ref, g, o_scr, l_scr, row0, dilation, first):
    def rows(start, n):
        if dilation == 1:
            return pl.ds(start, n)
        return pl.ds(start, n, stride=dilation)

    nk = A_BLOCK if first else 2 * A_BLOCK
    k0 = row0 if first else row0 - A_BLOCK * dilation
    q = q_ref[rows(row0, A_BLOCK), :] * (A_HEAD_DIM ** -0.5)
    k = k_ref[rows(k0, nk), :].astype(BF16)
    v = v_ref[rows(k0, nk), :]
    head0 = lax.broadcasted_iota(jnp.int32, (1, A_PAIR), 1) < A_HEAD_DIM
    res, mx = [], []
    for h in range(2):
        mine = head0 if h == 0 else jnp.logical_not(head0)
        qh = jnp.where(mine, q, 0.0).astype(BF16)
        s = lax.dot_general(qh, k, NT_DIMS, preferred_element_type=F32)
        if first:
            s = s + bias_ref[g, h, :, A_BLOCK:]
        else:
            s = s + bias_ref[g, h]
        m = jnp.max(s, axis=-1, keepdims=True)
        p = jnp.exp(s - m).astype(BF16)
        ve = jnp.where(mine, v, 1.0).astype(BF16)
        res.append(jnp.dot(p, ve, preferred_element_type=F32))
        mx.append(m)
    acc = jnp.where(head0, res[0], res[1])
    den = pltpu.roll(jnp.where(head0, res[1], res[0]), A_HEAD_DIM, 1)
    lse = jnp.where(head0, mx[0], mx[1]) + jnp.log(den)
    o_scr[g, rows(row0, A_BLOCK), :] = acc / den
    l_scr[g, rows(row0, A_BLOCK), :] = lse


def _a_attn_kernel(q0, k0, v0, q1, k1, v1, q2, k2, v2, bias_ref, o_ref, o_scr, l_scr):
    qkv = ((q0, k0, v0), (q1, k1, v1), (q2, k2, v2))
    for g, (_, dilation) in enumerate(A_PATTERNS):
        q_ref, k_ref, v_ref = qkv[g]
        nb = SEQ // dilation // A_BLOCK
        unit = functools.partial(_a_unit, q_ref, k_ref, v_ref, bias_ref, g, o_scr, l_scr,
                                 dilation=dilation)
        if nb == 1:
            def first_blocks(cb, carry, unit=unit):
                for u in range(A_UNROLL):
                    unit(row0=cb * A_UNROLL + u, first=True)
                return carry

            lax.fori_loop(0, dilation // A_UNROLL, first_blocks, 0)
        elif nb <= A_UNROLL:
            def subseq(c, carry, unit=unit, nb=nb, dilation=dilation):
                for i in range(nb):
                    unit(row0=c + i * (A_BLOCK * dilation), first=i == 0)
                return carry

            lax.fori_loop(0, dilation, subseq, 0)
        else:
            assert dilation == 1 and nb % A_UNROLL == 0
            for i in range(A_UNROLL):
                unit(row0=i * A_BLOCK, first=i == 0)

            def later_blocks(ib, carry, unit=unit):
                for u in range(A_UNROLL):
                    unit(row0=(ib * A_UNROLL + u) * A_BLOCK, first=False)
                return carry

            lax.fori_loop(1, nb // A_UNROLL, later_blocks, 0)

    rows_per_step = 256

    def merge(r, carry):
        sl = pl.ds(pl.multiple_of(r * rows_per_step, rows_per_step), rows_per_step)
        l0, l1, l2 = l_scr[0, sl, :], l_scr[1, sl, :], l_scr[2, sl, :]
        m = jnp.maximum(jnp.maximum(l0, l1), l2)
        w0, w1, w2 = jnp.exp(l0 - m), jnp.exp(l1 - m), jnp.exp(l2 - m)
        num = w0 * o_scr[0, sl, :] + w1 * o_scr[1, sl, :] + w2 * o_scr[2, sl, :]
        o_ref[sl, :] = (num / (w0 + w1 + w2)).astype(o_ref.dtype)
        return carry

    lax.fori_loop(0, SEQ // rows_per_step, merge, 0)


def _a_attn(proj, bias):
    b = proj.shape[0]
    n_pairs = A_GROUP_WIDTH // A_PAIR
    in_specs = []
    for g in range(A_GROUPS):
        for r in range(3):
            col = (g * 3 + r) * n_pairs
            in_specs.append(pl.BlockSpec((None, SEQ, A_PAIR),
                                         lambda bi, hp, col=col: (bi, 0, col + hp)))
    in_specs.append(pl.BlockSpec((A_GROUPS, 2, A_BLOCK, 2 * A_BLOCK), lambda bi, hp: (0, hp, 0, 0)))
    return pl.pallas_call(
        _a_attn_kernel, grid=(b, n_pairs), in_specs=in_specs,
        out_specs=pl.BlockSpec((None, SEQ, A_PAIR), lambda bi, hp: (bi, 0, hp)),
        out_shape=jax.ShapeDtypeStruct((b, SEQ, A_GROUP_WIDTH), BF16),
        scratch_shapes=[pltpu.VMEM((A_GROUPS, SEQ, A_PAIR), F32),
                        pltpu.VMEM((A_GROUPS, SEQ, A_PAIR), F32)],
        compiler_params=_params(2), name="a_attn")(*([proj] * 9), bias)


B_IN_TILE = 512
B_PREV = 8


def _chunk_cumsum(x, axis):
    pos = lax.broadcasted_iota(jnp.int32, x.shape, axis) & (B_CHUNK - 1)
    step = 1
    while step < B_CHUNK:
        x = x + jnp.where(pos >= step, pltpu.roll(x, step, axis), 0.0)
        step *= 2
    return x


def _b_in_kernel(x_ref, g_ref, wqkv_ref, wz_ref, wab_ref, wabt_ref, conv_ref,
                 alog_ref, dtb_ref, alogt_ref, dtbt_ref,
                 qkv_ref, z_ref, gates_ref, gct_ref, pre_scr):
    tiles_per_seq = SEQ // B_IN_TILE
    seq_start = pl.program_id(0) % tiles_per_seq == 0
    h = _rms(x_ref[...], g_ref[...]).astype(BF16)

    @pl.when(seq_start)
    def _():
        pre_scr[pl.ds(0, B_PREV), :] = jnp.zeros((B_PREV, 3 * B_WIDTH), F32)

    @pl.when(jnp.logical_not(seq_start))
    def _():
        pre_scr[pl.ds(0, B_PREV), :] = pre_scr[pl.ds(B_IN_TILE, B_PREV), :]

    pre_scr[pl.ds(B_PREV, B_IN_TILE), :] = jnp.dot(h, wqkv_ref[...], preferred_element_type=F32)
    z_ref[...] = jnp.dot(h, wz_ref[...], preferred_element_type=F32)

    conv = conv_ref[B_CONV - 1:B_CONV, :] * pre_scr[pl.ds(B_PREV, B_IN_TILE), :]
    for j in range(B_CONV - 1):
        conv = conv + conv_ref[j:j + 1, :] * pre_scr[pl.ds(B_PREV - (B_CONV - 1) + j, B_IN_TILE), :]
    act = _silu(conv)
    for hd in range(2 * B_HEADS):
        sl = slice(hd * B_HEAD_DIM, (hd + 1) * B_HEAD_DIM)
        t = act[:, sl]
        n = t * lax.rsqrt(jnp.sum(t * t, axis=-1, keepdims=True) + NORM_EPS)
        if hd < B_HEADS:
            n = n * (B_HEAD_DIM ** -0.5)
        qkv_ref[:, sl] = n
    qkv_ref[:, 2 * B_WIDTH:] = act[:, 2 * B_WIDTH:]

    ab = jnp.dot(h, wab_ref[...], preferred_element_type=F32)
    abt = lax.dot_general(wabt_ref[...], h, NT_DIMS, preferred_element_type=F32)
    glog = -jnp.exp(alog_ref[...]) * _softplus(ab + dtb_ref[...])
    glogt = -jnp.exp(alogt_ref[...]) * _softplus(abt + dtbt_ref[...])
    gc = _chunk_cumsum(glog, 0)
    gct = _chunk_cumsum(glogt, 1)
    is_decay = lax.broadcasted_iota(jnp.int32, (1, 2 * B_HEADS), 1) < B_HEADS
    gates_ref[...] = jnp.where(is_decay, gc, jax.nn.sigmoid(ab))
    for j in range(B_IN_TILE // B_CHUNK):
        gct_ref[j] = gct[:B_HEADS, j * B_CHUNK:(j + 1) * B_CHUNK]


def _b_in(x, g, wqkv, wz, wab, conv_w, a_log, dt_bias):
    t = x.shape[0]
    tm = B_IN_TILE
    wabt = wab.T
    pad = jnp.zeros((1, B_HEADS), F32)
    a_log = jnp.concatenate([a_log.reshape(1, B_HEADS).astype(F32), pad], axis=1)
    dt_bias = jnp.concatenate([dt_bias.reshape(1, B_HEADS).astype(F32), pad], axis=1)
    return pl.pallas_call(
        _b_in_kernel, grid=(t // tm,),
        in_specs=[pl.BlockSpec((tm, D_MODEL), lambda i: (i, 0)),
                  _const_spec((1, D_MODEL)), _const_spec((D_MODEL, 3 * B_WIDTH)),
                  _const_spec((D_MODEL, B_WIDTH)), _const_spec((D_MODEL, 2 * B_HEADS)),
                  _const_spec((2 * B_HEADS, D_MODEL)), _const_spec((B_CONV, 3 * B_WIDTH)),
                  _const_spec((1, 2 * B_HEADS)), _const_spec((1, 2 * B_HEADS)),
                  _const_spec((2 * B_HEADS, 1)), _const_spec((2 * B_HEADS, 1))],
        out_specs=[pl.BlockSpec((tm, 3 * B_WIDTH), lambda i: (i, 0)),
                   pl.BlockSpec((tm, B_WIDTH), lambda i: (i, 0)),
                   pl.BlockSpec((tm, 2 * B_HEADS), lambda i: (i, 0)),
                   pl.BlockSpec((tm // B_CHUNK, B_HEADS, B_CHUNK), lambda i: (i, 0, 0))],
        out_shape=[jax.ShapeDtypeStruct((t, 3 * B_WIDTH), F32),
                   jax.ShapeDtypeStruct((t, B_WIDTH), F32),
                   jax.ShapeDtypeStruct((t, 2 * B_HEADS), F32),
                   jax.ShapeDtypeStruct((t // B_CHUNK, B_HEADS, B_CHUNK), F32)],
        scratch_shapes=[pltpu.VMEM((B_PREV + tm, 3 * B_WIDTH), F32)],
        compiler_params=pltpu.CompilerParams(dimension_semantics=("arbitrary",),
                                             vmem_limit_bytes=VMEM_LIMIT),
        name="b_in")(x, g, wqkv, wz, wab, wabt, conv_w, a_log, dt_bias, a_log.T, dt_bias.T)


B_STEP_CHUNKS = 4
B_PAIRS = B_HEADS // 2
B_PAIR_W = 2 * B_CHUNK
INV_ROWS = 4
SCAN_SEQS = 4


def _chunk_decay(gates_ref, gct_ref, c, rows, h, keep):
    gc = gates_ref[rows, h:h + 1]
    return jnp.where(keep, jnp.exp(jnp.minimum(gc - gct_ref[c, h:h + 1, :], 0.0)), 0.0)


def _b_low_kernel(k_ref, gates_ref, gct_ref, l_ref):
    strict = (lax.broadcasted_iota(jnp.int32, (B_CHUNK, B_CHUNK), 0)
              > lax.broadcasted_iota(jnp.int32, (B_CHUNK, B_CHUNK), 1))

    def chunk(c, carry):
        rows = pl.ds(pl.multiple_of(c * B_CHUNK, B_CHUNK), B_CHUNK)
        pair = []
        for h in range(B_HEADS):
            k = k_ref[rows, h * B_HEAD_DIM:(h + 1) * B_HEAD_DIM]
            beta = gates_ref[rows, B_HEADS + h:B_HEADS + h + 1]
            kk = lax.dot_general((k * beta).astype(BF16), k.astype(BF16), NT_DIMS,
                                 preferred_element_type=F32)
            pair.append(kk * _chunk_decay(gates_ref, gct_ref, c, rows, h, strict))
            if h % 2 == 1:
                l_ref[h // 2, rows, :] = jnp.concatenate(pair, axis=-1)
                pair = []
        return carry

    lax.fori_loop(0, B_STEP_CHUNKS, chunk, 0)


def _b_low(qkv, gates, gct):
    t = qkv.shape[0]
    tm = B_STEP_CHUNKS * B_CHUNK
    return pl.pallas_call(
        _b_low_kernel, grid=(t // tm,),
        in_specs=[pl.BlockSpec((tm, B_WIDTH), lambda i: (i, 1)),
                  pl.BlockSpec((tm, 2 * B_HEADS), lambda i: (i, 0)),
                  pl.BlockSpec((B_STEP_CHUNKS, B_HEADS, B_CHUNK), lambda i: (i, 0, 0))],
        out_specs=pl.BlockSpec((B_PAIRS, tm, B_PAIR_W), lambda i: (0, i, 0)),
        out_shape=jax.ShapeDtypeStruct((B_PAIRS, t, B_PAIR_W), F32),
        compiler_params=_params(1), name="b_low")(qkv, gates, gct)


def _b_inv_kernel(l_ref, t_ref, lt_scr, xt_scr):
    def to_problem_lanes(i, carry):
        slab = jnp.concatenate(
            [l_ref[pr, pl.ds(i, N_CHUNKS, stride=B_CHUNK), :] for pr in range(B_PAIRS)], axis=0)
        slab_t = slab.T
        lt_scr[0, i] = slab_t[:B_CHUNK]
        lt_scr[1, i] = slab_t[B_CHUNK:]
        return carry

    lax.fori_loop(0, B_CHUNK, to_problem_lanes, 0)

    col = lax.broadcasted_iota(jnp.int32, (B_CHUNK, N_CHUNKS * B_PAIRS), 0)
    for half in range(2):
        def row_group(ib, carry, half=half):
            i0 = ib * INV_ROWS
            accs = tuple(jnp.where(col == i0 + r, 1.0, 0.0) for r in range(INV_ROWS))

            def earlier_row(j, accs):
                xj = xt_scr[half, j]
                return tuple(accs[r] - lt_scr[half, i0 + r, pl.ds(j, 1), :] * xj
                             for r in range(INV_ROWS))

            accs = list(lax.fori_loop(0, i0, earlier_row, accs))
            for r in range(INV_ROWS):
                for r2 in range(r):
                    accs[r] = accs[r] - lt_scr[half, i0 + r, pl.ds(i0 + r2, 1), :] * accs[r2]
                xt_scr[half, i0 + r] = accs[r]
            return carry

        lax.fori_loop(0, B_CHUNK // INV_ROWS, row_group, 0)

    def from_problem_lanes(i, carry):
        slab_t = jnp.concatenate([xt_scr[0, i], xt_scr[1, i]], axis=0).T
        for pr in range(B_PAIRS):
            t_ref[pr, pl.ds(i, N_CHUNKS, stride=B_CHUNK), :] = (
                slab_t[pr * N_CHUNKS:(pr + 1) * N_CHUNKS])
        return carry

    lax.fori_loop(0, B_CHUNK, from_problem_lanes, 0)


def _b_inv(low):
    t = low.shape[1]
    block = pl.BlockSpec((B_PAIRS, SEQ, B_PAIR_W), lambda b: (0, b, 0))
    scr = pltpu.VMEM((2, B_CHUNK, B_CHUNK, N_CHUNKS * B_PAIRS), F32)
    return pl.pallas_call(
        _b_inv_kernel, grid=(t // SEQ,), in_specs=[block], out_specs=block,
        out_shape=jax.ShapeDtypeStruct(low.shape, F32), scratch_shapes=[scr, scr],
        compiler_params=_params(1), name="b_inv")(low)


def _b_prep_kernel(q_ref, k_ref, v_ref, gates_ref, gct_ref, t_ref,
                   u_ref, w_ref, qd_ref, kt_ref, attn_ref):
    tril = (lax.broadcasted_iota(jnp.int32, (B_CHUNK, B_CHUNK), 0)
            >= lax.broadcasted_iota(jnp.int32, (B_CHUNK, B_CHUNK), 1))

    def chunk(c, carry):
        rows = pl.ds(pl.multiple_of(c * B_CHUNK, B_CHUNK), B_CHUNK)
        last = pl.ds(c * B_CHUNK + B_CHUNK - 1, 1)
        pair = []
        for h in range(B_HEADS):
            sl = slice(h * B_HEAD_DIM, (h + 1) * B_HEAD_DIM)
            q, k, v = q_ref[rows, sl], k_ref[rows, sl], v_ref[rows, sl]
            gc = gates_ref[rows, h:h + 1]
            beta = gates_ref[rows, B_HEADS + h:B_HEADS + h + 1]
            egc = jnp.exp(gc)
            kb = k * beta
            qk = lax.dot_general(q.astype(BF16), k.astype(BF16), NT_DIMS, preferred_element_type=F32)
            tinv = t_ref[h // 2, rows, (h % 2) * B_CHUNK:(h % 2 + 1) * B_CHUNK].astype(BF16)
            u_ref[rows, sl] = jnp.dot(tinv, (v * beta).astype(BF16),
                                      preferred_element_type=F32).astype(u_ref.dtype)
            w_ref[rows, sl] = jnp.dot(tinv, (kb * egc).astype(BF16),
                                      preferred_element_type=F32).astype(w_ref.dtype)
            qd_ref[rows, sl] = (q * egc).astype(qd_ref.dtype)
            kt_ref[rows, sl] = (k * jnp.exp(gates_ref[last, h:h + 1] - gc)).astype(kt_ref.dtype)
            pair.append(qk * _chunk_decay(gates_ref, gct_ref, c, rows, h, tril))
            if h % 2 == 1:
                attn_ref[rows, (h - 1) * B_CHUNK:(h + 1) * B_CHUNK] = jnp.concatenate(
                    pair, axis=-1).astype(attn_ref.dtype)
                pair = []
        return carry

    lax.fori_loop(0, B_STEP_CHUNKS, chunk, 0)


def _b_prep(qkv, gates, gct, tinv):
    t = qkv.shape[0]
    tm = B_STEP_CHUNKS * B_CHUNK
    wide = lambda col: pl.BlockSpec((tm, B_WIDTH), lambda i, col=col: (i, col))
    narrow = pl.BlockSpec((tm, B_HEADS * B_CHUNK), lambda i: (i, 0))
    out_wide = jax.ShapeDtypeStruct((t, B_WIDTH), BF16)
    return pl.pallas_call(
        _b_prep_kernel, grid=(t // tm,),
        in_specs=[wide(0), wide(1), wide(2),
                  pl.BlockSpec((tm, 2 * B_HEADS), lambda i: (i, 0)),
                  pl.BlockSpec((B_STEP_CHUNKS, B_HEADS, B_CHUNK), lambda i: (i, 0, 0)),
                  pl.BlockSpec((B_PAIRS, tm, B_PAIR_W), lambda i: (0, i, 0))],
        out_specs=[wide(0)] * 4 + [narrow],
        out_shape=[out_wide] * 4 + [jax.ShapeDtypeStruct((t, B_HEADS * B_CHUNK), BF16)],
        compiler_params=_params(1), name="b_prep")(qkv, qkv, qkv, gates, gct, tinv)


def _b_scan_kernel(u_ref, w_ref, qd_ref, kt_ref, attn_ref, gates_ref, z_ref, nw_ref,
                   y_ref, state):
    @pl.when(pl.program_id(1) == 0)
    def _():
        state[...] = jnp.zeros_like(state)

    nw = nw_ref[...]
    zero = jnp.zeros((B_CHUNK, B_HEAD_DIM), BF16)
    for b in range(state.shape[0]):
        for hp in range(B_HEADS // 2):
            v_new = []
            q_s = []
            for h in (2 * hp, 2 * hp + 1):
                sl = slice(h * B_HEAD_DIM, (h + 1) * B_HEAD_DIM)
                s_bf = state[b, h].astype(BF16)
                wq = jnp.concatenate([w_ref[b, :, sl], qd_ref[b, :, sl]], axis=0)
                r = jnp.dot(wq, s_bf, preferred_element_type=F32)
                vn = (u_ref[b, :, sl].astype(F32) - r[:B_CHUNK]).astype(BF16)
                v_new.append(vn)
                q_s.append(r[B_CHUNK:])
                chunk_decay = jnp.exp(gates_ref[b, B_CHUNK - 1:B_CHUNK, h:h + 1])
                state[b, h] = state[b, h] * chunk_decay + lax.dot_general(
                    kt_ref[b, :, sl], vn, TN_DIMS, preferred_element_type=F32)
            vblock = jnp.concatenate(
                [jnp.concatenate([v_new[0], zero], axis=1),
                 jnp.concatenate([zero, v_new[1]], axis=1)], axis=0)
            intra = jnp.dot(attn_ref[b, :, 2 * hp * B_CHUNK:(2 * hp + 2) * B_CHUNK], vblock,
                            preferred_element_type=F32)
            for j, h in enumerate((2 * hp, 2 * hp + 1)):
                sl = slice(h * B_HEAD_DIM, (h + 1) * B_HEAD_DIM)
                o = q_s[j] + intra[:, j * B_HEAD_DIM:(j + 1) * B_HEAD_DIM]
                y = _rms(o, nw) * _silu(z_ref[b, :, sl])
                y_ref[b, :, sl] = y.astype(y_ref.dtype)


def _b_scan(u, w, qd, kt, attn, gates, z, norm_w):
    t = u.shape[0]
    b = t // SEQ
    seqs = math.gcd(b, SCAN_SEQS)
    per_seq = lambda a: a.reshape(b, SEQ, a.shape[-1])
    spec = lambda width: pl.BlockSpec((seqs, B_CHUNK, width), lambda bi, n: (bi, n, 0))
    wide = spec(B_WIDTH)
    y = pl.pallas_call(
        _b_scan_kernel, grid=(b // seqs, N_CHUNKS),
        in_specs=[wide, wide, wide, wide, spec(B_HEADS * B_CHUNK), spec(2 * B_HEADS),
                  wide, _const_spec((1, B_HEAD_DIM))],
        out_specs=wide,
        out_shape=jax.ShapeDtypeStruct((b, SEQ, B_WIDTH), BF16),
        scratch_shapes=[pltpu.VMEM((seqs, B_HEADS, B_HEAD_DIM, B_HEAD_DIM), F32)],
        compiler_params=pltpu.CompilerParams(dimension_semantics=("parallel", "arbitrary"),
                                             vmem_limit_bytes=VMEM_LIMIT),
        name="b_scan")(*(per_seq(a) for a in (u, w, qd, kt, attn, gates, z)), norm_w)
    return y.reshape(t, B_WIDTH)


def kernel(x, norm_g, ffn_w_gate, ffn_w_up, ffn_w_down, rel_bias, a_w_in, a_w_out,
           b_w_in, b_conv_w, b_a_log, b_dt_bias, b_norm_w, b_w_out, final_g):
    batch, seq, d = x.shape
    assert (seq, d) == (SEQ, D_MODEL)
    depth = norm_g.shape[0]
    t = batch * seq
    x = x.reshape(t, d)
    bias = _a_bias_tables(rel_bias)
    norm_g = norm_g.astype(F32)
    for i in range(depth):
        j = i // 2
        x = _ffn(x, norm_g[i, 0][None], ffn_w_gate[i, 0].astype(BF16), ffn_w_up[i, 0].astype(BF16),
                 ffn_w_down[i, 0].astype(BF16))
        g_mix = norm_g[i, 1][None]
        if i % 2 == 0:
            proj = _a_in(x, g_mix, a_w_in[j].astype(BF16))
            o = _a_attn(proj.reshape(batch, seq, A_IN_WIDTH), bias)
            x = _proj_res(x, o.reshape(t, A_GROUP_WIDTH), a_w_out[j].astype(BF16))
        else:
            w_in = b_w_in[j].astype(BF16)
            qkv, z, gates, gct = _b_in(x, g_mix, w_in[:, :3 * B_WIDTH], w_in[:, 3 * B_WIDTH:4 * B_WIDTH],
                                       w_in[:, 4 * B_WIDTH:], b_conv_w[j].astype(F32),
                                       b_a_log[j], b_dt_bias[j])
            tinv = _b_inv(_b_low(qkv, gates, gct))
            u, w, qd, kt, attn = _b_prep(qkv, gates, gct, tinv)
            y = _b_scan(u, w, qd, kt, attn, gates, z, b_norm_w[j].astype(F32)[None])
            x = _proj_res(x, y, b_w_out[j].astype(BF16))
        last = i == depth - 1
        x = _ffn(x, norm_g[i, 2][None], ffn_w_gate[i, 1].astype(BF16), ffn_w_up[i, 1].astype(BF16),
                 ffn_w_down[i, 1].astype(BF16), final_g=final_g.astype(F32)[None] if last else None)
    return x.reshape(batch, seq, d)
```

```python
import functools
import math

import numpy as np
import jax
import jax.numpy as jnp
from jax import lax
from jax.experimental import pallas as pl
from jax.experimental.pallas import tpu as pltpu

F32 = jnp.float32
BF16 = jnp.bfloat16

D_MODEL = 1024
SEQ = 2048
D_FF = 2816
NORM_EPS = 1e-6
MACARON_WEIGHT = 0.5

A_PATTERNS = ((128, 1), (512, 4), (2048, 16))
A_GROUPS = 3
A_HEADS = 8
A_HEAD_DIM = 64
A_GROUP_WIDTH = A_HEADS * A_HEAD_DIM
A_IN_WIDTH = A_GROUPS * 3 * A_GROUP_WIDTH
A_BLOCK = 128
A_PAIR = 2 * A_HEAD_DIM
A_UNROLL = 4
NEG_INF = -1e30
NUM_BUCKETS = 32
MAX_DISTANCE = 2048

B_HEADS = 8
B_HEAD_DIM = 128
B_WIDTH = B_HEADS * B_HEAD_DIM
B_CONV = 4
B_CHUNK = 64
N_CHUNKS = SEQ // B_CHUNK

VMEM_LIMIT = 56 * 1024 * 1024

NT_DIMS = (((1,), (1,)), ((), ()))
TN_DIMS = (((0,), (0,)), ((), ()))


def _const_spec(shape):
    zeros = (0,) * len(shape)
    return pl.BlockSpec(shape, lambda *_: zeros, pipeline_mode=pl.Buffered(1))


def _params(n_axes):
    return pltpu.CompilerParams(dimension_semantics=("parallel",) * n_axes,
                                vmem_limit_bytes=VMEM_LIMIT)


def _rms(x, g):
    ms = jnp.mean(x * x, axis=-1, keepdims=True)
    return x * lax.rsqrt(ms + NORM_EPS) * g


def _silu(x):
    return x * jax.nn.sigmoid(x)


def _softplus(x):
    return jnp.maximum(x, 0.0) + jnp.log1p(jnp.exp(-jnp.abs(x)))


def _ffn_kernel(x_ref, g_ref, wg_ref, wu_ref, wd_ref, *rest, final):
    o_ref = rest[-1]
    x = x_ref[...]
    h = _rms(x, g_ref[...]).astype(BF16)
    gate = jnp.dot(h, wg_ref[...], preferred_element_type=F32)
    up = jnp.dot(h, wu_ref[...], preferred_element_type=F32)
    a = (_silu(gate) * up).astype(BF16)
    y = x + MACARON_WEIGHT * jnp.dot(a, wd_ref[...], preferred_element_type=F32)
    if final:
        y = _rms(y, rest[0][...])
    o_ref[...] = y


def _ffn(x, g, wg, wu, wd, final_g=None, tm=512):
    t = x.shape[0]
    row = pl.BlockSpec((tm, D_MODEL), lambda i: (i, 0))
    in_specs = [row, _const_spec((1, D_MODEL)), _const_spec((D_MODEL, D_FF)),
                _const_spec((D_MODEL, D_FF)), _const_spec((D_FF, D_MODEL))]
    args = [x, g, wg, wu, wd]
    if final_g is not None:
        in_specs.append(_const_spec((1, D_MODEL)))
        args.append(final_g)
    return pl.pallas_call(
        functools.partial(_ffn_kernel, final=final_g is not None),
        grid=(t // tm,), in_specs=in_specs, out_specs=row,
        out_shape=jax.ShapeDtypeStruct((t, D_MODEL), F32),
        compiler_params=_params(1), name="ffn")(*args)


def _proj_res_kernel(x_ref, y_ref, w_ref, o_ref):
    o_ref[...] = x_ref[...] + jnp.dot(y_ref[...], w_ref[...], preferred_element_type=F32)


def _proj_res(x, y, w, tm=1024):
    t, k = y.shape
    return pl.pallas_call(
        _proj_res_kernel, grid=(t // tm,),
        in_specs=[pl.BlockSpec((tm, D_MODEL), lambda i: (i, 0)),
                  pl.BlockSpec((tm, k), lambda i: (i, 0)),
                  _const_spec((k, D_MODEL))],
        out_specs=pl.BlockSpec((tm, D_MODEL), lambda i: (i, 0)),
        out_shape=jax.ShapeDtypeStruct((t, D_MODEL), F32),
        compiler_params=_params(1), name="proj_res")(x, y, w)


def _a_in_kernel(x_ref, g_ref, w_ref, o_ref):
    h = _rms(x_ref[...], g_ref[...]).astype(BF16)
    o_ref[...] = jnp.dot(h, w_ref[...], preferred_element_type=F32)


def _a_in(x, g, w, tm=512):
    t = x.shape[0]
    return pl.pallas_call(
        _a_in_kernel, grid=(t // tm,),
        in_specs=[pl.BlockSpec((tm, D_MODEL), lambda i: (i, 0)),
                  _const_spec((1, D_MODEL)), _const_spec((D_MODEL, A_IN_WIDTH))],
        out_specs=pl.BlockSpec((tm, A_IN_WIDTH), lambda i: (i, 0)),
        out_shape=jax.ShapeDtypeStruct((t, A_IN_WIDTH), F32),
        compiler_params=_params(1), name="a_in")(x, g, w)


def _t5_bucket_np(distance):
    max_exact = NUM_BUCKETS // 2
    n = distance.astype(np.float32)
    large = np.float32(max_exact) + (
        np.log(np.maximum(n, np.float32(1.0)) / np.float32(max_exact))
        / np.float32(math.log(MAX_DISTANCE / max_exact)) * np.float32(NUM_BUCKETS - max_exact))
    large = np.minimum(large.astype(np.int32), NUM_BUCKETS - 1)
    return np.where(distance < max_exact, distance, large).astype(np.int32)


def _a_bias_rows(rel_bias):
    m = np.arange(2 * A_BLOCK)
    back = A_BLOCK - m
    rows = []
    for g, (window, dilation) in enumerate(A_PATTERNS):
        assert window // dilation == A_BLOCK
        bucket = _t5_bucket_np(np.maximum(back, 0) * dilation)
        b = rel_bias.astype(F32)[:, g * A_HEADS:(g + 1) * A_HEADS][bucket]
        rows.append(jnp.where((back >= 0)[:, None], b, NEG_INF).T)
    return jnp.stack(rows)


def _a_units(q_ref, k_ref, v_ref, bias_scr, g, o_ref, l_ref, units, stride):
    def rows(start, n):
        return pl.ds(start, n) if stride == 1 else pl.ds(start, n, stride=stride)

    head0 = lax.broadcasted_iota(jnp.int32, (1, A_PAIR), 1) < A_HEAD_DIM
    mine = (head0, jnp.logical_not(head0))
    values, scores = [], []
    for base, first in units:
        nk = A_BLOCK if first else 2 * A_BLOCK
        k0 = base if first else base - A_BLOCK * stride
        q = q_ref[rows(base, A_BLOCK), :] * (A_HEAD_DIM ** -0.5)
        k = k_ref[rows(k0, nk), :].astype(BF16)
        values.append(v_ref[rows(k0, nk), :])
        for h in range(2):
            qh = jnp.where(mine[h], q, 0.0).astype(BF16)
            s = lax.dot_general(qh, k, NT_DIMS, preferred_element_type=F32)
            scores.append(s + (bias_scr[g, h, :, A_BLOCK:] if first else bias_scr[g, h]))
    maxes = [jnp.max(s, axis=-1, keepdims=True) for s in scores]
    probs = [jnp.exp(s - m).astype(BF16) for s, m in zip(scores, maxes)]
    results = []
    for n, p in enumerate(probs):
        ve = jnp.where(mine[n % 2], values[n // 2], 1.0).astype(BF16)
        results.append(jnp.dot(p, ve, preferred_element_type=F32))
    for u, (base, _) in enumerate(units):
        r0, r1 = results[2 * u], results[2 * u + 1]
        den = pltpu.roll(jnp.where(head0, r1, r0), A_HEAD_DIM, 1)
        o_ref[rows(base, A_BLOCK), :] = jnp.where(head0, r0, r1) / den
        l_ref[rows(base, A_BLOCK), :] = (jnp.where(head0, maxes[2 * u], maxes[2 * u + 1])
                                         + jnp.log(den))


def _a_attn_kernel(q0, k0, v0, q1, k1, v1, q2, k2, v2, brow_ref, o_ref,
                   bias_scr, sub_scr, o_scr, l_scr, tok_scr):
    pair = pl.program_id(1)
    for g in range(A_GROUPS):
        for h in range(2):
            row = brow_ref[g, pl.ds(2 * pair + h, 1), :]
            bias_scr[g, h] = pltpu.roll(jnp.broadcast_to(row, (A_BLOCK, 2 * A_BLOCK)), 0, 1,
                                        stride=1, stride_axis=0)

    n_blocks = SEQ // A_BLOCK
    quarter = SEQ // 4

    units0 = functools.partial(_a_units, q0, k0, v0, bias_scr, 0, o_scr.at[0], l_scr.at[0], stride=1)
    units0([(u * A_BLOCK, u == 0) for u in range(A_UNROLL)])

    def later_blocks(ib, carry):
        units0([(pl.multiple_of((ib * A_UNROLL + u) * A_BLOCK, A_BLOCK), False)
                for u in range(A_UNROLL)])
        return carry

    lax.fori_loop(1, n_blocks // A_UNROLL, later_blocks, 0)

    units1 = functools.partial(_a_units, q1, k1, v1, bias_scr, 1, o_scr.at[1], l_scr.at[1], stride=4)

    def subsequence(c, carry):
        units1([(c + 4 * A_BLOCK * i, i == 0) for i in range(n_blocks // 4)])
        return carry

    lax.fori_loop(0, 4, subsequence, 0)

    for r, src in enumerate((q2, k2, v2)):
        for b in range(4):
            sub_scr[r, pl.ds(b * quarter, quarter), :] = src[pl.ds(b, quarter, stride=4), :]
    units2 = functools.partial(_a_units, sub_scr.at[0], sub_scr.at[1], sub_scr.at[2], bias_scr, 2,
                               o_scr.at[2], l_scr.at[2], stride=4)

    def subsequences(a, carry):
        units2([(b * quarter + a, True) for b in range(4)])
        return carry

    lax.fori_loop(0, 4, subsequences, 0)
    for n, scr in enumerate((o_scr, l_scr)):
        for b in range(4):
            tok_scr[n, pl.ds(b, quarter, stride=4), :] = scr[2, pl.ds(b * quarter, quarter), :]

    rows_per_step = 256

    def merge(r, carry):
        sl = pl.ds(pl.multiple_of(r * rows_per_step, rows_per_step), rows_per_step)
        l0, l1, l2 = l_scr[0, sl, :], l_scr[1, sl, :], tok_scr[1, sl, :]
        m = jnp.maximum(jnp.maximum(l0, l1), l2)
        w0, w1, w2 = jnp.exp(l0 - m), jnp.exp(l1 - m), jnp.exp(l2 - m)
        num = w0 * o_scr[0, sl, :] + w1 * o_scr[1, sl, :] + w2 * tok_scr[0, sl, :]
        o_ref[sl, :] = (num / (w0 + w1 + w2)).astype(o_ref.dtype)
        return carry

    lax.fori_loop(0, SEQ // rows_per_step, merge, 0)


def _a_attn(proj, bias_rows):
    b = proj.shape[0]
    n_pairs = A_GROUP_WIDTH // A_PAIR
    in_specs = []
    for g in range(A_GROUPS):
        for r in range(3):
            col = (g * 3 + r) * n_pairs
            in_specs.append(pl.BlockSpec((None, SEQ, A_PAIR),
                                         lambda bi, hp, col=col: (bi, 0, col + hp)))
    in_specs.append(_const_spec((A_GROUPS, A_HEADS, 2 * A_BLOCK)))
    return pl.pallas_call(
        _a_attn_kernel, grid=(b, n_pairs), in_specs=in_specs,
        out_specs=pl.BlockSpec((None, SEQ, A_PAIR), lambda bi, hp: (bi, 0, hp)),
        out_shape=jax.ShapeDtypeStruct((b, SEQ, A_GROUP_WIDTH), BF16),
        scratch_shapes=[pltpu.VMEM((A_GROUPS, 2, A_BLOCK, 2 * A_BLOCK), F32),
                        pltpu.VMEM((3, SEQ, A_PAIR), F32),
                        pltpu.VMEM((A_GROUPS, SEQ, A_PAIR), F32),
                        pltpu.VMEM((A_GROUPS, SEQ, A_PAIR), F32),
                        pltpu.VMEM((2, SEQ, A_PAIR), F32)],
        compiler_params=_params(2), name="a_attn")(*([proj] * 9), bias_rows)


B_IN_TILE = 512
B_PREV = 8


def _chunk_cumsum(x, axis):
    pos = lax.broadcasted_iota(jnp.int32, x.shape, axis) & (B_CHUNK - 1)
    step = 1
    while step < B_CHUNK:
        x = x + jnp.where(pos >= step, pltpu.roll(x, step, axis), 0.0)
        step *= 2
    return x


def _b_in_kernel(x_ref, g_ref, wqkv_ref, wz_ref, wab_ref, wabt_ref, conv_ref,
                 alog_ref, dtb_ref, alogt_ref, dtbt_ref,
                 qkv_ref, z_ref, gates_ref, gct_ref, pre_scr):
    tiles_per_seq = SEQ // B_IN_TILE
    seq_start = pl.program_id(0) % tiles_per_seq == 0
    h = _rms(x_ref[...], g_ref[...]).astype(BF16)

    @pl.when(seq_start)
    def _():
        pre_scr[pl.ds(0, B_PREV), :] = jnp.zeros((B_PREV, 3 * B_WIDTH), F32)

    @pl.when(jnp.logical_not(seq_start))
    def _():
        pre_scr[pl.ds(0, B_PREV), :] = pre_scr[pl.ds(B_IN_TILE, B_PREV), :]

    pre_scr[pl.ds(B_PREV, B_IN_TILE), :] = jnp.dot(h, wqkv_ref[...], preferred_element_type=F32)
    z_ref[...] = jnp.dot(h, wz_ref[...], preferred_element_type=F32).astype(z_ref.dtype)

    conv = conv_ref[B_CONV - 1:B_CONV, :] * pre_scr[pl.ds(B_PREV, B_IN_TILE), :]
    for j in range(B_CONV - 1):
        conv = conv + conv_ref[j:j + 1, :] * pre_scr[pl.ds(B_PREV - (B_CONV - 1) + j, B_IN_TILE), :]
    act = _silu(conv)
    for hd in range(2 * B_HEADS):
        sl = slice(hd * B_HEAD_DIM, (hd + 1) * B_HEAD_DIM)
        t = act[:, sl]
        n = t * lax.rsqrt(jnp.sum(t * t, axis=-1, keepdims=True) + NORM_EPS)
        if hd < B_HEADS:
            n = n * (B_HEAD_DIM ** -0.5)
        qkv_ref[:, sl] = n.astype(qkv_ref.dtype)
    qkv_ref[:, 2 * B_WIDTH:] = act[:, 2 * B_WIDTH:].astype(qkv_ref.dtype)

    ab = jnp.dot(h, wab_ref[...], preferred_element_type=F32)
    abt = lax.dot_general(wabt_ref[...], h, NT_DIMS, preferred_element_type=F32)
    glog = -jnp.exp(alog_ref[...]) * _softplus(ab + dtb_ref[...])
    glogt = -jnp.exp(alogt_ref[...]) * _softplus(abt + dtbt_ref[...])
    gc = _chunk_cumsum(glog, 0)
    gct = _chunk_cumsum(glogt, 1)
    is_decay = lax.broadcasted_iota(jnp.int32, (1, 2 * B_HEADS), 1) < B_HEADS
    gates_ref[...] = jnp.where(is_decay, gc, jax.nn.sigmoid(ab))
    for j in range(B_IN_TILE // B_CHUNK):
        gct_ref[j] = gct[:B_HEADS, j * B_CHUNK:(j + 1) * B_CHUNK]


def _b_in(x, g, wqkv, wz, wab, conv_w, a_log, dt_bias):
    t = x.shape[0]
    tm = B_IN_TILE
    wabt = wab.T
    pad = jnp.zeros((1, B_HEADS), F32)
    a_log = jnp.concatenate([a_log.reshape(1, B_HEADS).astype(F32), pad], axis=1)
    dt_bias = jnp.concatenate([dt_bias.reshape(1, B_HEADS).astype(F32), pad], axis=1)
    return pl.pallas_call(
        _b_in_kernel, grid=(t // tm,),
        in_specs=[pl.BlockSpec((tm, D_MODEL), lambda i: (i, 0)),
                  _const_spec((1, D_MODEL)), _const_spec((D_MODEL, 3 * B_WIDTH)),
                  _const_spec((D_MODEL, B_WIDTH)), _const_spec((D_MODEL, 2 * B_HEADS)),
                  _const_spec((2 * B_HEADS, D_MODEL)), _const_spec((B_CONV, 3 * B_WIDTH)),
                  _const_spec((1, 2 * B_HEADS)), _const_spec((1, 2 * B_HEADS)),
                  _const_spec((2 * B_HEADS, 1)), _const_spec((2 * B_HEADS, 1))],
        out_specs=[pl.BlockSpec((tm, 3 * B_WIDTH), lambda i: (i, 0)),
                   pl.BlockSpec((tm, B_WIDTH), lambda i: (i, 0)),
                   pl.BlockSpec((tm, 2 * B_HEADS), lambda i: (i, 0)),
                   pl.BlockSpec((tm // B_CHUNK, B_HEADS, B_CHUNK), lambda i: (i, 0, 0))],
        out_shape=[jax.ShapeDtypeStruct((t, 3 * B_WIDTH), BF16),
                   jax.ShapeDtypeStruct((t, B_WIDTH), BF16),
                   jax.ShapeDtypeStruct((t, 2 * B_HEADS), F32),
                   jax.ShapeDtypeStruct((t // B_CHUNK, B_HEADS, B_CHUNK), F32)],
        scratch_shapes=[pltpu.VMEM((B_PREV + tm, 3 * B_WIDTH), F32)],
        compiler_params=pltpu.CompilerParams(dimension_semantics=("arbitrary",),
                                             vmem_limit_bytes=VMEM_LIMIT),
        name="b_in")(x, g, wqkv, wz, wab, wabt, conv_w, a_log, dt_bias, a_log.T, dt_bias.T)


B_STEP_CHUNKS = 8
B_STEP = B_STEP_CHUNKS * B_CHUNK
B_PAIRS = B_HEADS // 2
B_PAIR_W = 2 * B_CHUNK
INV_ROWS = 8
XPOSE_ROWS = 4
SCAN_SEQS = 4


def _chunk_rows(c):
    return pl.ds(c, B_CHUNK, stride=B_STEP_CHUNKS)


def _chunk_decay(gates_ref, gct_ref, c, rows, h, keep):
    gc = gates_ref[rows, h:h + 1]
    return jnp.where(keep, jnp.exp(jnp.minimum(gc - gct_ref[c, h:h + 1, :], 0.0)), 0.0)


def _b_low_kernel(k_ref, gates_ref, gct_ref, l_ref):
    strict = (lax.broadcasted_iota(jnp.int32, (B_CHUNK, B_CHUNK), 0)
              > lax.broadcasted_iota(jnp.int32, (B_CHUNK, B_CHUNK), 1))

    def chunk(c, carry):
        rows = pl.ds(pl.multiple_of(c * B_CHUNK, B_CHUNK), B_CHUNK)
        pair = []
        for h in range(B_HEADS):
            k = k_ref[rows, h * B_HEAD_DIM:(h + 1) * B_HEAD_DIM]
            beta = gates_ref[rows, B_HEADS + h:B_HEADS + h + 1]
            kk = lax.dot_general((k.astype(F32) * beta).astype(BF16), k, NT_DIMS,
                                 preferred_element_type=F32)
            pair.append(kk * _chunk_decay(gates_ref, gct_ref, c, rows, h, strict))
            if h % 2 == 1:
                l_ref[h // 2, _chunk_rows(c), :] = jnp.concatenate(pair, axis=-1)
                pair = []
        return carry

    lax.fori_loop(0, B_STEP_CHUNKS, chunk, 0)


def _b_low(qkv, gates, gct):
    t = qkv.shape[0]
    return pl.pallas_call(
        _b_low_kernel, grid=(t // B_STEP,),
        in_specs=[pl.BlockSpec((B_STEP, B_WIDTH), lambda i: (i, 1)),
                  pl.BlockSpec((B_STEP, 2 * B_HEADS), lambda i: (i, 0)),
                  pl.BlockSpec((B_STEP_CHUNKS, B_HEADS, B_CHUNK), lambda i: (i, 0, 0))],
        out_specs=pl.BlockSpec((B_PAIRS, B_STEP, B_PAIR_W), lambda i: (0, i, 0)),
        out_shape=jax.ShapeDtypeStruct((B_PAIRS, t, B_PAIR_W), F32),
        compiler_params=_params(1), name="b_low")(qkv, gates, gct)


def _b_inv_kernel(l_ref, t_ref, lt_scr, xt_scr):
    steps = SEQ // B_STEP

    def tiles(i):
        return [(pr, pl.ds(pl.multiple_of(s * B_STEP + i * B_STEP_CHUNKS, B_STEP_CHUNKS), B_STEP_CHUNKS))
                for pr in range(B_PAIRS) for s in range(steps)]

    def to_problem_lanes(ib, carry):
        for i in [ib * XPOSE_ROWS + r for r in range(XPOSE_ROWS)]:
            slab_t = jnp.concatenate([l_ref[pr, rows, :] for pr, rows in tiles(i)], axis=0).T
            lt_scr[0, i] = slab_t[:B_CHUNK]
            lt_scr[1, i] = slab_t[B_CHUNK:]
        return carry

    lax.fori_loop(0, B_CHUNK // XPOSE_ROWS, to_problem_lanes, 0)

    xt_scr[...] = jnp.zeros_like(xt_scr)
    sub = lax.broadcasted_iota(jnp.int32, (INV_ROWS, N_CHUNKS * B_PAIRS), 0)
    groups = B_CHUNK // INV_ROWS

    def column_block(hc, carry):
        half, cb = hc // groups, hc % groups
        cols = pl.ds(pl.multiple_of(cb * INV_ROWS, INV_ROWS), INV_ROWS)

        def row_group(ib, carry2):
            i0 = ib * INV_ROWS
            accs = tuple(jnp.where(jnp.logical_and(ib == cb, sub == r), 1.0, 0.0)
                         for r in range(INV_ROWS))

            def earlier_rows(jb, accs):
                for jj in range(INV_ROWS):
                    j = jb * INV_ROWS + jj
                    xj = xt_scr[half, j, cols, :]
                    accs = tuple(accs[r] - lt_scr[half, i0 + r, pl.ds(j, 1), :] * xj
                                 for r in range(INV_ROWS))
                return accs

            accs = list(lax.fori_loop(cb, ib, earlier_rows, accs))
            for r in range(INV_ROWS):
                for r2 in range(r):
                    accs[r] = accs[r] - lt_scr[half, i0 + r, pl.ds(i0 + r2, 1), :] * accs[r2]
                xt_scr[half, i0 + r, cols, :] = accs[r]
            return carry2

        lax.fori_loop(cb, groups, row_group, 0)
        return carry

    lax.fori_loop(0, 2 * groups, column_block, 0)

    def from_problem_lanes(ib, carry):
        for i in [ib * XPOSE_ROWS + r for r in range(XPOSE_ROWS)]:
            slab_t = jnp.concatenate([xt_scr[0, i], xt_scr[1, i]], axis=0).T
            for n, (pr, rows) in enumerate(tiles(i)):
                t_ref[pr, rows, :] = slab_t[n * B_STEP_CHUNKS:(n + 1) * B_STEP_CHUNKS]
        return carry

    lax.fori_loop(0, B_CHUNK // XPOSE_ROWS, from_problem_lanes, 0)


def _b_inv(low):
    t = low.shape[1]
    block = pl.BlockSpec((B_PAIRS, SEQ, B_PAIR_W), lambda b: (0, b, 0))
    scr = pltpu.VMEM((2, B_CHUNK, B_CHUNK, N_CHUNKS * B_PAIRS), F32)
    return pl.pallas_call(
        _b_inv_kernel, grid=(t // SEQ,), in_specs=[block], out_specs=block,
        out_shape=jax.ShapeDtypeStruct(low.shape, F32), scratch_shapes=[scr, scr],
        compiler_params=_params(1), name="b_inv")(low)


def _b_prep_kernel(q_ref, k_ref, v_ref, gates_ref, gct_ref, t_ref,
                   u_ref, w_ref, qd_ref, kt_ref, attn_ref):
    tril = (lax.broadcasted_iota(jnp.int32, (B_CHUNK, B_CHUNK), 0)
            >= lax.broadcasted_iota(jnp.int32, (B_CHUNK, B_CHUNK), 1))

    def chunk(c, carry):
        rows = pl.ds(pl.multiple_of(c * B_CHUNK, B_CHUNK), B_CHUNK)
        last = pl.ds(c * B_CHUNK + B_CHUNK - 1, 1)
        pair = []
        for h in range(B_HEADS):
            sl = slice(h * B_HEAD_DIM, (h + 1) * B_HEAD_DIM)
            q, k, v = q_ref[rows, sl], k_ref[rows, sl], v_ref[rows, sl]
            q32, k32, v32 = q.astype(F32), k.astype(F32), v.astype(F32)
            gc = gates_ref[rows, h:h + 1]
            beta = gates_ref[rows, B_HEADS + h:B_HEADS + h + 1]
            egc = jnp.exp(gc)
            kb = k32 * beta
            qk = lax.dot_general(q, k, NT_DIMS, preferred_element_type=F32)
            if h % 2 == 0:
                tinv_pair = t_ref[h // 2, _chunk_rows(c), :]
            tinv = tinv_pair[:, (h % 2) * B_CHUNK:(h % 2 + 1) * B_CHUNK].astype(BF16)
            u_ref[rows, sl] = jnp.dot(tinv, (v32 * beta).astype(BF16),
                                      preferred_element_type=F32).astype(u_ref.dtype)
            w_ref[rows, sl] = jnp.dot(tinv, (kb * egc).astype(BF16),
                                      preferred_element_type=F32).astype(w_ref.dtype)
            qd_ref[rows, sl] = (q32 * egc).astype(qd_ref.dtype)
            kt_ref[rows, sl] = (k32 * jnp.exp(gates_ref[last, h:h + 1] - gc)).astype(kt_ref.dtype)
            pair.append(qk * _chunk_decay(gates_ref, gct_ref, c, rows, h, tril))
            if h % 2 == 1:
                attn_ref[rows, (h - 1) * B_CHUNK:(h + 1) * B_CHUNK] = jnp.concatenate(
                    pair, axis=-1).astype(attn_ref.dtype)
                pair = []
        return carry

    lax.fori_loop(0, B_STEP_CHUNKS, chunk, 0)


def _b_prep(qkv, gates, gct, tinv):
    t = qkv.shape[0]
    tm = B_STEP
    wide = lambda col: pl.BlockSpec((tm, B_WIDTH), lambda i, col=col: (i, col))
    narrow = pl.BlockSpec((tm, B_HEADS * B_CHUNK), lambda i: (i, 0))
    out_wide = jax.ShapeDtypeStruct((t, B_WIDTH), BF16)
    return pl.pallas_call(
        _b_prep_kernel, grid=(t // tm,),
        in_specs=[wide(0), wide(1), wide(2),
                  pl.BlockSpec((tm, 2 * B_HEADS), lambda i: (i, 0)),
                  pl.BlockSpec((B_STEP_CHUNKS, B_HEADS, B_CHUNK), lambda i: (i, 0, 0)),
                  pl.BlockSpec((B_PAIRS, tm, B_PAIR_W), lambda i: (0, i, 0))],
        out_specs=[wide(0)] * 4 + [narrow],
        out_shape=[out_wide] * 4 + [jax.ShapeDtypeStruct((t, B_HEADS * B_CHUNK), BF16)],
        compiler_params=_params(1), name="b_prep")(qkv, qkv, qkv, gates, gct, tinv)


def _b_scan_kernel(u_ref, w_ref, qd_ref, kt_ref, attn_ref, gates_ref, z_ref, nw_ref,
                   y_ref, state):
    @pl.when(pl.program_id(1) == 0)
    def _():
        state[...] = jnp.zeros_like(state)

    nw = nw_ref[...]
    zero = jnp.zeros((B_CHUNK, B_HEAD_DIM), BF16)
    seqs = state.shape[0]
    heads = [(b, h) for b in range(seqs) for h in range(B_HEADS)]
    cols = lambda h: slice(h * B_HEAD_DIM, (h + 1) * B_HEAD_DIM)
    ws_qs = {}
    for b, h in heads:
        wq = jnp.concatenate([w_ref[b, :, cols(h)], qd_ref[b, :, cols(h)]], axis=0)
        ws_qs[b, h] = jnp.dot(wq, state[b, h].astype(BF16), preferred_element_type=F32)
    v_new = {}
    for b, h in heads:
        v_new[b, h] = (u_ref[b, :, cols(h)].astype(F32) - ws_qs[b, h][:B_CHUNK]).astype(BF16)
    for b, h in heads:
        chunk_decay = jnp.exp(gates_ref[b, B_CHUNK - 1:B_CHUNK, h:h + 1])
        state[b, h] = state[b, h] * chunk_decay + lax.dot_general(
            kt_ref[b, :, cols(h)], v_new[b, h], TN_DIMS, preferred_element_type=F32)
    for b in range(seqs):
        for hp in range(B_HEADS // 2):
            h0, h1 = 2 * hp, 2 * hp + 1
            vblock = jnp.concatenate(
                [jnp.concatenate([v_new[b, h0], zero], axis=1),
                 jnp.concatenate([zero, v_new[b, h1]], axis=1)], axis=0)
            intra = jnp.dot(attn_ref[b, :, h0 * B_CHUNK:(h1 + 1) * B_CHUNK], vblock,
                            preferred_element_type=F32)
            for j, h in enumerate((h0, h1)):
                o = ws_qs[b, h][B_CHUNK:] + intra[:, j * B_HEAD_DIM:(j + 1) * B_HEAD_DIM]
                y = _rms(o, nw) * _silu(z_ref[b, :, cols(h)].astype(F32))
                y_ref[b, :, cols(h)] = y.astype(y_ref.dtype)


def _b_scan(u, w, qd, kt, attn, gates, z, norm_w):
    t = u.shape[0]
    b = t // SEQ
    seqs = math.gcd(b, SCAN_SEQS)
    per_seq = lambda a: a.reshape(b, SEQ, a.shape[-1])
    spec = lambda width: pl.BlockSpec((seqs, B_CHUNK, width), lambda bi, n: (bi, n, 0))
    wide = spec(B_WIDTH)
    y = pl.pallas_call(
        _b_scan_kernel, grid=(b // seqs, N_CHUNKS),
        in_specs=[wide, wide, wide, wide, spec(B_HEADS * B_CHUNK), spec(2 * B_HEADS),
                  wide, _const_spec((1, B_HEAD_DIM))],
        out_specs=wide,
        out_shape=jax.ShapeDtypeStruct((b, SEQ, B_WIDTH), BF16),
        scratch_shapes=[pltpu.VMEM((seqs, B_HEADS, B_HEAD_DIM, B_HEAD_DIM), F32)],
        compiler_params=pltpu.CompilerParams(dimension_semantics=("parallel", "arbitrary"),
                                             vmem_limit_bytes=VMEM_LIMIT),
        name="b_scan")(*(per_seq(a) for a in (u, w, qd, kt, attn, gates, z)), norm_w)
    return y.reshape(t, B_WIDTH)


def kernel(x, norm_g, ffn_w_gate, ffn_w_up, ffn_w_down, rel_bias, a_w_in, a_w_out,
           b_w_in, b_conv_w, b_a_log, b_dt_bias, b_norm_w, b_w_out, final_g):
    batch, seq, d = x.shape
    assert (seq, d) == (SEQ, D_MODEL)
    depth = norm_g.shape[0]
    t = batch * seq
    x = x.reshape(t, d)
    bias = _a_bias_rows(rel_bias)
    norm_g = norm_g.astype(F32)
    for i in range(depth):
        j = i // 2
        x = _ffn(x, norm_g[i, 0][None], ffn_w_gate[i, 0].astype(BF16), ffn_w_up[i, 0].astype(BF16),
                 ffn_w_down[i, 0].astype(BF16))
        g_mix = norm_g[i, 1][None]
        if i % 2 == 0:
            proj = _a_in(x, g_mix, a_w_in[j].astype(BF16))
            o = _a_attn(proj.reshape(batch, seq, A_IN_WIDTH), bias)
            x = _proj_res(x, o.reshape(t, A_GROUP_WIDTH), a_w_out[j].astype(BF16))
        else:
            w_in = b_w_in[j].astype(BF16)
            qkv, z, gates, gct = _b_in(x, g_mix, w_in[:, :3 * B_WIDTH], w_in[:, 3 * B_WIDTH:4 * B_WIDTH],
                                       w_in[:, 4 * B_WIDTH:], b_conv_w[j].astype(F32),
                                       b_a_log[j], b_dt_bias[j])
            tinv = _b_inv(_b_low(qkv, gates, gct))
            u, w, qd, kt, attn = _b_prep(qkv, gates, gct, tinv)
            y = _b_scan(u, w, qd, kt, attn, gates, z, b_norm_w[j].astype(F32)[None])
            x = _proj_res(x, y, b_w_out[j].astype(BF16))
        last = i == depth - 1
        x = _ffn(x, norm_g[i, 2][None], ffn_w_gate[i, 1].astype(BF16), ffn_w_up[i, 1].astype(BF16),
                 ffn_w_down[i, 1].astype(BF16), final_g=final_g.astype(F32)[None] if last else None)
    return x.reshape(batch, seq, d)
```

```python
import functools
import math

import numpy as np
import jax
import jax.numpy as jnp
from jax import lax
from jax.experimental import pallas as pl
from jax.experimental.pallas import tpu as pltpu

F32 = jnp.float32
BF16 = jnp.bfloat16

D_MODEL = 1024
SEQ = 2048
D_FF = 2816
NORM_EPS = 1e-6
MACARON_WEIGHT = 0.5

A_PATTERNS = ((128, 1), (512, 4), (2048, 16))
A_GROUPS = 3
A_HEADS = 8
A_HEAD_DIM = 64
A_GROUP_WIDTH = A_HEADS * A_HEAD_DIM
A_IN_WIDTH = A_GROUPS * 3 * A_GROUP_WIDTH
A_BLOCK = 128
A_PAIR = 2 * A_HEAD_DIM
A_UNROLL = 8
NEG_INF = -1e30
NUM_BUCKETS = 32
MAX_DISTANCE = 2048

B_HEADS = 8
B_HEAD_DIM = 128
B_WIDTH = B_HEADS * B_HEAD_DIM
B_CONV = 4
B_CHUNK = 64
N_CHUNKS = SEQ // B_CHUNK

VMEM_LIMIT = 56 * 1024 * 1024

NT_DIMS = (((1,), (1,)), ((), ()))
TN_DIMS = (((0,), (0,)), ((), ()))


def _const_spec(shape):
    zeros = (0,) * len(shape)
    return pl.BlockSpec(shape, lambda *_: zeros, pipeline_mode=pl.Buffered(1))


def _params(n_axes):
    return pltpu.CompilerParams(dimension_semantics=("parallel",) * n_axes,
                                vmem_limit_bytes=VMEM_LIMIT)


def _rms(x, g):
    ms = jnp.mean(x * x, axis=-1, keepdims=True)
    return x * lax.rsqrt(ms + NORM_EPS) * g


def _silu(x):
    return x * jax.nn.sigmoid(x)


def _softplus(x):
    return jnp.maximum(x, 0.0) + jnp.log1p(jnp.exp(-jnp.abs(x)))


def _gated_head_norm(o_ref, z_ref, nw):
    parts = []
    for h in range(B_HEADS):
        sl = slice(h * B_HEAD_DIM, (h + 1) * B_HEAD_DIM)
        y = _rms(o_ref[:, sl].astype(F32), nw) * _silu(z_ref[:, sl].astype(F32))
        parts.append(y.astype(BF16))
    return jnp.concatenate(parts, axis=1)


def _ffn_kernel(x_ref, *refs, mixer, final):
    refs = list(refs)
    o_ref = refs.pop()
    x = x_ref[...]
    if mixer == "a":
        y_ref, wo_ref = refs[:2]
        refs = refs[2:]
        x = x + jnp.dot(y_ref[...], wo_ref[...], preferred_element_type=F32)
    elif mixer == "b":
        y_ref, z_ref, nw_ref, wo_ref = refs[:4]
        refs = refs[4:]
        y = _gated_head_norm(y_ref, z_ref, nw_ref[...])
        x = x + jnp.dot(y, wo_ref[...], preferred_element_type=F32)
    g_ref, wg_ref, wu_ref, wd_ref = refs[:4]
    h = _rms(x, g_ref[...]).astype(BF16)
    gate = jnp.dot(h, wg_ref[...], preferred_element_type=F32)
    up = jnp.dot(h, wu_ref[...], preferred_element_type=F32)
    a = (_silu(gate) * up).astype(BF16)
    y = x + MACARON_WEIGHT * jnp.dot(a, wd_ref[...], preferred_element_type=F32)
    if final:
        y = _rms(y, refs[4][...])
    o_ref[...] = y


def _ffn(x, g, wg, wu, wd, final_g=None, mixer=None, mixer_args=(), tm=512):
    t = x.shape[0]
    row = pl.BlockSpec((tm, D_MODEL), lambda i: (i, 0))
    in_specs, args = [row], [x]
    for a in mixer_args:
        if a.shape[0] == t:
            in_specs.append(pl.BlockSpec((tm, a.shape[1]), lambda i: (i, 0)))
        else:
            in_specs.append(_const_spec(a.shape))
        args.append(a)
    in_specs += [_const_spec((1, D_MODEL)), _const_spec((D_MODEL, D_FF)),
                 _const_spec((D_MODEL, D_FF)), _const_spec((D_FF, D_MODEL))]
    args += [g, wg, wu, wd]
    if final_g is not None:
        in_specs.append(_const_spec((1, D_MODEL)))
        args.append(final_g)
    return pl.pallas_call(
        functools.partial(_ffn_kernel, mixer=mixer, final=final_g is not None),
        grid=(t // tm,), in_specs=in_specs, out_specs=row,
        out_shape=jax.ShapeDtypeStruct((t, D_MODEL), F32),
        compiler_params=_params(1), name="ffn")(*args)


def _a_in_kernel(x_ref, g_ref, w_ref, o_ref):
    h = _rms(x_ref[...], g_ref[...]).astype(BF16)
    o_ref[...] = jnp.dot(h, w_ref[...], preferred_element_type=F32)


def _a_in(x, g, w, tm=512):
    t = x.shape[0]
    return pl.pallas_call(
        _a_in_kernel, grid=(t // tm,),
        in_specs=[pl.BlockSpec((tm, D_MODEL), lambda i: (i, 0)),
                  _const_spec((1, D_MODEL)), _const_spec((D_MODEL, A_IN_WIDTH))],
        out_specs=pl.BlockSpec((tm, A_IN_WIDTH), lambda i: (i, 0)),
        out_shape=jax.ShapeDtypeStruct((t, A_IN_WIDTH), F32),
        compiler_params=_params(1), name="a_in")(x, g, w)


def _t5_bucket_np(distance):
    max_exact = NUM_BUCKETS // 2
    n = distance.astype(np.float32)
    large = np.float32(max_exact) + (
        np.log(np.maximum(n, np.float32(1.0)) / np.float32(max_exact))
        / np.float32(math.log(MAX_DISTANCE / max_exact)) * np.float32(NUM_BUCKETS - max_exact))
    large = np.minimum(large.astype(np.int32), NUM_BUCKETS - 1)
    return np.where(distance < max_exact, distance, large).astype(np.int32)


def _a_bias_rows(rel_bias):
    m = np.arange(2 * A_BLOCK)
    back = A_BLOCK - m
    rows = []
    for g, (window, dilation) in enumerate(A_PATTERNS):
        assert window // dilation == A_BLOCK
        bucket = _t5_bucket_np(np.maximum(back, 0) * dilation)
        b = rel_bias.astype(F32)[:, g * A_HEADS:(g + 1) * A_HEADS][bucket]
        rows.append(jnp.where((back >= 0)[:, None], b, NEG_INF).T)
    return jnp.stack(rows)


def _a_units(q_ref, k_ref, v_ref, bias_scr, g, o_ref, l_ref, units, stride):
    def rows(start, n):
        return pl.ds(start, n) if stride == 1 else pl.ds(start, n, stride=stride)

    head0 = lax.broadcasted_iota(jnp.int32, (1, A_PAIR), 1) < A_HEAD_DIM
    mine = (head0, jnp.logical_not(head0))
    values, scores = [], []
    for base, first in units:
        nk = A_BLOCK if first else 2 * A_BLOCK
        k0 = base if first else base - A_BLOCK * stride
        q = q_ref[rows(base, A_BLOCK), :] * (A_HEAD_DIM ** -0.5)
        k = k_ref[rows(k0, nk), :].astype(BF16)
        values.append(v_ref[rows(k0, nk), :])
        for h in range(2):
            qh = jnp.where(mine[h], q, 0.0).astype(BF16)
            s = lax.dot_general(qh, k, NT_DIMS, preferred_element_type=F32)
            scores.append(s + (bias_scr[g, h, :, A_BLOCK:] if first else bias_scr[g, h]))
    maxes = [jnp.max(s, axis=-1, keepdims=True) for s in scores]
    probs = [jnp.exp(s - m).astype(BF16) for s, m in zip(scores, maxes)]
    results = []
    for n, p in enumerate(probs):
        ve = jnp.where(mine[n % 2], values[n // 2], 1.0).astype(BF16)
        results.append(jnp.dot(p, ve, preferred_element_type=F32))
    for u, (base, _) in enumerate(units):
        r0, r1 = results[2 * u], results[2 * u + 1]
        den = pltpu.roll(jnp.where(head0, r1, r0), A_HEAD_DIM, 1)
        o_ref[rows(base, A_BLOCK), :] = jnp.where(head0, r0, r1) / den
        l_ref[rows(base, A_BLOCK), :] = (jnp.where(head0, maxes[2 * u], maxes[2 * u + 1])
                                         + jnp.log(den))


def _a_attn_kernel(q0, k0, v0, q1, k1, v1, q2, k2, v2, brow_ref, o_ref,
                   bias_scr, sub_scr, o_scr, l_scr, tok_scr):
    pair = pl.program_id(1)
    for g in range(A_GROUPS):
        for h in range(2):
            row = brow_ref[g, pl.ds(2 * pair + h, 1), :]
            bias_scr[g, h] = pltpu.roll(jnp.broadcast_to(row, (A_BLOCK, 2 * A_BLOCK)), 0, 1,
                                        stride=1, stride_axis=0)

    n_blocks = SEQ // A_BLOCK
    quarter = SEQ // 4

    units0 = functools.partial(_a_units, q0, k0, v0, bias_scr, 0, o_scr.at[0], l_scr.at[0], stride=1)
    units0([(u * A_BLOCK, u == 0) for u in range(A_UNROLL)])

    def later_blocks(ib, carry):
        units0([(pl.multiple_of((ib * A_UNROLL + u) * A_BLOCK, A_BLOCK), False)
                for u in range(A_UNROLL)])
        return carry

    lax.fori_loop(1, n_blocks // A_UNROLL, later_blocks, 0)

    units1 = functools.partial(_a_units, q1, k1, v1, bias_scr, 1, o_scr.at[1], l_scr.at[1], stride=4)

    per_sub = n_blocks // 4
    subs = max(A_UNROLL // per_sub, 1)

    def subsequence(cb, carry):
        units1([(cb * subs + s + 4 * A_BLOCK * i, i == 0) for s in range(subs) for i in range(per_sub)])
        return carry

    lax.fori_loop(0, 4 // subs, subsequence, 0)

    for r, src in enumerate((q2, k2, v2)):
        for b in range(4):
            sub_scr[r, pl.ds(b * quarter, quarter), :] = src[pl.ds(b, quarter, stride=4), :]
    units2 = functools.partial(_a_units, sub_scr.at[0], sub_scr.at[1], sub_scr.at[2], bias_scr, 2,
                               o_scr.at[2], l_scr.at[2], stride=4)

    rows_per_body = A_UNROLL // 4

    def subsequences(ab, carry):
        units2([(b * quarter + ab * rows_per_body + a, True) for a in range(rows_per_body) for b in range(4)])
        return carry

    lax.fori_loop(0, 4 // rows_per_body, subsequences, 0)
    for n, scr in enumerate((o_scr, l_scr)):
        for b in range(4):
            tok_scr[n, pl.ds(b, quarter, stride=4), :] = scr[2, pl.ds(b * quarter, quarter), :]

    rows_per_step = 256

    def merge(r, carry):
        sl = pl.ds(pl.multiple_of(r * rows_per_step, rows_per_step), rows_per_step)
        l0, l1, l2 = l_scr[0, sl, :], l_scr[1, sl, :], tok_scr[1, sl, :]
        m = jnp.maximum(jnp.maximum(l0, l1), l2)
        w0, w1, w2 = jnp.exp(l0 - m), jnp.exp(l1 - m), jnp.exp(l2 - m)
        num = w0 * o_scr[0, sl, :] + w1 * o_scr[1, sl, :] + w2 * tok_scr[0, sl, :]
        o_ref[sl, :] = (num / (w0 + w1 + w2)).astype(o_ref.dtype)
        return carry

    lax.fori_loop(0, SEQ // rows_per_step, merge, 0)


def _a_attn(proj, bias_rows):
    b = proj.shape[0]
    n_pairs = A_GROUP_WIDTH // A_PAIR
    in_specs = []
    for g in range(A_GROUPS):
        for r in range(3):
            col = (g * 3 + r) * n_pairs
            in_specs.append(pl.BlockSpec((None, SEQ, A_PAIR),
                                         lambda bi, hp, col=col: (bi, 0, col + hp)))
    in_specs.append(_const_spec((A_GROUPS, A_HEADS, 2 * A_BLOCK)))
    return pl.pallas_call(
        _a_attn_kernel, grid=(b, n_pairs), in_specs=in_specs,
        out_specs=pl.BlockSpec((None, SEQ, A_PAIR), lambda bi, hp: (bi, 0, hp)),
        out_shape=jax.ShapeDtypeStruct((b, SEQ, A_GROUP_WIDTH), BF16),
        scratch_shapes=[pltpu.VMEM((A_GROUPS, 2, A_BLOCK, 2 * A_BLOCK), F32),
                        pltpu.VMEM((3, SEQ, A_PAIR), F32),
                        pltpu.VMEM((A_GROUPS, SEQ, A_PAIR), F32),
                        pltpu.VMEM((A_GROUPS, SEQ, A_PAIR), F32),
                        pltpu.VMEM((2, SEQ, A_PAIR), F32)],
        compiler_params=_params(2), name="a_attn")(*([proj] * 9), bias_rows)


B_IN_TILE = 512
B_PREV = 8


def _chunk_cumsum(x, axis):
    pos = lax.broadcasted_iota(jnp.int32, x.shape, axis) & (B_CHUNK - 1)
    step = 1
    while step < B_CHUNK:
        x = x + jnp.where(pos >= step, pltpu.roll(x, step, axis), 0.0)
        step *= 2
    return x


def _b_in_kernel(x_ref, g_ref, wqkv_ref, wz_ref, wab_ref, wabt_ref, conv_ref,
                 alog_ref, dtb_ref, alogt_ref, dtbt_ref,
                 qkv_ref, z_ref, gates_ref, gct_ref, pre_scr):
    tiles_per_seq = SEQ // B_IN_TILE
    seq_start = pl.program_id(0) % tiles_per_seq == 0
    h = _rms(x_ref[...], g_ref[...]).astype(BF16)

    @pl.when(seq_start)
    def _():
        pre_scr[pl.ds(0, B_PREV), :] = jnp.zeros((B_PREV, 3 * B_WIDTH), F32)

    @pl.when(jnp.logical_not(seq_start))
    def _():
        pre_scr[pl.ds(0, B_PREV), :] = pre_scr[pl.ds(B_IN_TILE, B_PREV), :]

    pre_scr[pl.ds(B_PREV, B_IN_TILE), :] = jnp.dot(h, wqkv_ref[...], preferred_element_type=F32)
    z_ref[...] = jnp.dot(h, wz_ref[...], preferred_element_type=F32).astype(z_ref.dtype)

    conv = conv_ref[B_CONV - 1:B_CONV, :] * pre_scr[pl.ds(B_PREV, B_IN_TILE), :]
    for j in range(B_CONV - 1):
        conv = conv + conv_ref[j:j + 1, :] * pre_scr[pl.ds(B_PREV - (B_CONV - 1) + j, B_IN_TILE), :]
    act = _silu(conv)
    for hd in range(2 * B_HEADS):
        sl = slice(hd * B_HEAD_DIM, (hd + 1) * B_HEAD_DIM)
        t = act[:, sl]
        n = t * lax.rsqrt(jnp.sum(t * t, axis=-1, keepdims=True) + NORM_EPS)
        if hd < B_HEADS:
            n = n * (B_HEAD_DIM ** -0.5)
        qkv_ref[:, sl] = n.astype(qkv_ref.dtype)
    qkv_ref[:, 2 * B_WIDTH:] = act[:, 2 * B_WIDTH:].astype(qkv_ref.dtype)

    ab = jnp.dot(h, wab_ref[...], preferred_element_type=F32)
    abt = lax.dot_general(wabt_ref[...], h, NT_DIMS, preferred_element_type=F32)
    glog = -jnp.exp(alog_ref[...]) * _softplus(ab + dtb_ref[...])
    glogt = -jnp.exp(alogt_ref[...]) * _softplus(abt + dtbt_ref[...])
    gc = _chunk_cumsum(glog, 0)
    gct = _chunk_cumsum(glogt, 1)
    is_decay = lax.broadcasted_iota(jnp.int32, (1, 2 * B_HEADS), 1) < B_HEADS
    gates_ref[...] = jnp.where(is_decay, gc, jax.nn.sigmoid(ab))
    for j in range(B_IN_TILE // B_CHUNK):
        gct_ref[j] = gct[:B_HEADS, j * B_CHUNK:(j + 1) * B_CHUNK]


def _b_in(x, g, wqkv, wz, wab, conv_w, a_log, dt_bias):
    t = x.shape[0]
    tm = B_IN_TILE
    wabt = wab.T
    pad = jnp.zeros((1, B_HEADS), F32)
    a_log = jnp.concatenate([a_log.reshape(1, B_HEADS).astype(F32), pad], axis=1)
    dt_bias = jnp.concatenate([dt_bias.reshape(1, B_HEADS).astype(F32), pad], axis=1)
    return pl.pallas_call(
        _b_in_kernel, grid=(t // tm,),
        in_specs=[pl.BlockSpec((tm, D_MODEL), lambda i: (i, 0)),
                  _const_spec((1, D_MODEL)), _const_spec((D_MODEL, 3 * B_WIDTH)),
                  _const_spec((D_MODEL, B_WIDTH)), _const_spec((D_MODEL, 2 * B_HEADS)),
                  _const_spec((2 * B_HEADS, D_MODEL)), _const_spec((B_CONV, 3 * B_WIDTH)),
                  _const_spec((1, 2 * B_HEADS)), _const_spec((1, 2 * B_HEADS)),
                  _const_spec((2 * B_HEADS, 1)), _const_spec((2 * B_HEADS, 1))],
        out_specs=[pl.BlockSpec((tm, 3 * B_WIDTH), lambda i: (i, 0)),
                   pl.BlockSpec((tm, B_WIDTH), lambda i: (i, 0)),
                   pl.BlockSpec((tm, 2 * B_HEADS), lambda i: (i, 0)),
                   pl.BlockSpec((tm // B_CHUNK, B_HEADS, B_CHUNK), lambda i: (i, 0, 0))],
        out_shape=[jax.ShapeDtypeStruct((t, 3 * B_WIDTH), BF16),
                   jax.ShapeDtypeStruct((t, B_WIDTH), BF16),
                   jax.ShapeDtypeStruct((t, 2 * B_HEADS), F32),
                   jax.ShapeDtypeStruct((t // B_CHUNK, B_HEADS, B_CHUNK), F32)],
        scratch_shapes=[pltpu.VMEM((B_PREV + tm, 3 * B_WIDTH), F32)],
        compiler_params=pltpu.CompilerParams(dimension_semantics=("arbitrary",),
                                             vmem_limit_bytes=VMEM_LIMIT),
        name="b_in")(x, g, wqkv, wz, wab, wabt, conv_w, a_log, dt_bias, a_log.T, dt_bias.T)


B_STEP_CHUNKS = 8
B_STEP = B_STEP_CHUNKS * B_CHUNK
B_PAIRS = B_HEADS // 2
B_PAIR_W = 2 * B_CHUNK
INV_ROWS = 8
XPOSE_ROWS = 4
SCAN_SEQS = 4


def _chunk_rows(c):
    return pl.ds(c, B_CHUNK, stride=B_STEP_CHUNKS)


def _chunk_decay(gates_ref, gct_ref, c, rows, h, keep):
    gc = gates_ref[rows, h:h + 1]
    return jnp.where(keep, jnp.exp(jnp.minimum(gc - gct_ref[c, h:h + 1, :], 0.0)), 0.0)


def _b_low_kernel(k_ref, gates_ref, gct_ref, l_ref):
    strict = (lax.broadcasted_iota(jnp.int32, (B_CHUNK, B_CHUNK), 0)
              > lax.broadcasted_iota(jnp.int32, (B_CHUNK, B_CHUNK), 1))

    def chunk(c, carry):
        rows = pl.ds(pl.multiple_of(c * B_CHUNK, B_CHUNK), B_CHUNK)
        pair = []
        for h in range(B_HEADS):
            k = k_ref[rows, h * B_HEAD_DIM:(h + 1) * B_HEAD_DIM]
            beta = gates_ref[rows, B_HEADS + h:B_HEADS + h + 1]
            kk = lax.dot_general((k.astype(F32) * beta).astype(BF16), k, NT_DIMS,
                                 preferred_element_type=F32)
            pair.append(kk * _chunk_decay(gates_ref, gct_ref, c, rows, h, strict))
            if h % 2 == 1:
                l_ref[h // 2, _chunk_rows(c), :] = jnp.concatenate(pair, axis=-1)
                pair = []
        return carry

    lax.fori_loop(0, B_STEP_CHUNKS, chunk, 0)


def _b_low(qkv, gates, gct):
    t = qkv.shape[0]
    return pl.pallas_call(
        _b_low_kernel, grid=(t // B_STEP,),
        in_specs=[pl.BlockSpec((B_STEP, B_WIDTH), lambda i: (i, 1)),
                  pl.BlockSpec((B_STEP, 2 * B_HEADS), lambda i: (i, 0)),
                  pl.BlockSpec((B_STEP_CHUNKS, B_HEADS, B_CHUNK), lambda i: (i, 0, 0))],
        out_specs=pl.BlockSpec((B_PAIRS, B_STEP, B_PAIR_W), lambda i: (0, i, 0)),
        out_shape=jax.ShapeDtypeStruct((B_PAIRS, t, B_PAIR_W), F32),
        compiler_params=_params(1), name="b_low")(qkv, gates, gct)


def _b_inv_kernel(l_ref, t_ref, lt_scr, xt_scr):
    steps = SEQ // B_STEP

    def tiles(i):
        return [(pr, pl.ds(pl.multiple_of(s * B_STEP + i * B_STEP_CHUNKS, B_STEP_CHUNKS), B_STEP_CHUNKS))
                for pr in range(B_PAIRS) for s in range(steps)]

    def to_problem_lanes(ib, carry):
        for i in [ib * XPOSE_ROWS + r for r in range(XPOSE_ROWS)]:
            slab_t = jnp.concatenate([l_ref[pr, rows, :] for pr, rows in tiles(i)], axis=0).T
            lt_scr[0, i] = slab_t[:B_CHUNK]
            lt_scr[1, i] = slab_t[B_CHUNK:]
        return carry

    lax.fori_loop(0, B_CHUNK // XPOSE_ROWS, to_problem_lanes, 0)

    xt_scr[...] = jnp.zeros_like(xt_scr)
    sub = lax.broadcasted_iota(jnp.int32, (INV_ROWS, N_CHUNKS * B_PAIRS), 0)
    groups = B_CHUNK // INV_ROWS

    def column_block(hc, carry):
        half, cb = hc // groups, hc % groups
        cols = pl.ds(pl.multiple_of(cb * INV_ROWS, INV_ROWS), INV_ROWS)

        def row_group(ib, carry2):
            i0 = ib * INV_ROWS
            accs = tuple(jnp.where(jnp.logical_and(ib == cb, sub == r), 1.0, 0.0)
                         for r in range(INV_ROWS))

            def earlier_rows(jb, accs):
                for jj in range(INV_ROWS):
                    j = jb * INV_ROWS + jj
                    xj = xt_scr[half, j, cols, :]
                    accs = tuple(accs[r] - lt_scr[half, i0 + r, pl.ds(j, 1), :] * xj
                                 for r in range(INV_ROWS))
                return accs

            accs = list(lax.fori_loop(cb, ib, earlier_rows, accs))
            for r in range(INV_ROWS):
                for r2 in range(r):
                    accs[r] = accs[r] - lt_scr[half, i0 + r, pl.ds(i0 + r2, 1), :] * accs[r2]
                xt_scr[half, i0 + r, cols, :] = accs[r]
            return carry2

        lax.fori_loop(cb, groups, row_group, 0)
        return carry

    lax.fori_loop(0, 2 * groups, column_block, 0)

    def from_problem_lanes(ib, carry):
        for i in [ib * XPOSE_ROWS + r for r in range(XPOSE_ROWS)]:
            slab_t = jnp.concatenate([xt_scr[0, i], xt_scr[1, i]], axis=0).T
            for n, (pr, rows) in enumerate(tiles(i)):
                t_ref[pr, rows, :] = slab_t[n * B_STEP_CHUNKS:(n + 1) * B_STEP_CHUNKS]
        return carry

    lax.fori_loop(0, B_CHUNK // XPOSE_ROWS, from_problem_lanes, 0)


def _b_inv(low):
    t = low.shape[1]
    block = pl.BlockSpec((B_PAIRS, SEQ, B_PAIR_W), lambda b: (0, b, 0))
    scr = pltpu.VMEM((2, B_CHUNK, B_CHUNK, N_CHUNKS * B_PAIRS), F32)
    return pl.pallas_call(
        _b_inv_kernel, grid=(t // SEQ,), in_specs=[block], out_specs=block,
        out_shape=jax.ShapeDtypeStruct(low.shape, F32), scratch_shapes=[scr, scr],
        compiler_params=_params(1), name="b_inv")(low)


def _b_prep_kernel(q_ref, k_ref, v_ref, gates_ref, gct_ref, t_ref,
                   u_ref, w_ref, qd_ref, kt_ref, attn_ref):
    tril = (lax.broadcasted_iota(jnp.int32, (B_CHUNK, B_CHUNK), 0)
            >= lax.broadcasted_iota(jnp.int32, (B_CHUNK, B_CHUNK), 1))

    def chunk(c, carry):
        rows = pl.ds(pl.multiple_of(c * B_CHUNK, B_CHUNK), B_CHUNK)
        last = pl.ds(c * B_CHUNK + B_CHUNK - 1, 1)
        pair = []
        for h in range(B_HEADS):
            sl = slice(h * B_HEAD_DIM, (h + 1) * B_HEAD_DIM)
            q, k, v = q_ref[rows, sl], k_ref[rows, sl], v_ref[rows, sl]
            q32, k32, v32 = q.astype(F32), k.astype(F32), v.astype(F32)
            gc = gates_ref[rows, h:h + 1]
            beta = gates_ref[rows, B_HEADS + h:B_HEADS + h + 1]
            egc = jnp.exp(gc)
            kb = k32 * beta
            qk = lax.dot_general(q, k, NT_DIMS, preferred_element_type=F32)
            if h % 2 == 0:
                tinv_pair = t_ref[h // 2, _chunk_rows(c), :]
            tinv = tinv_pair[:, (h % 2) * B_CHUNK:(h % 2 + 1) * B_CHUNK].astype(BF16)
            u_ref[rows, sl] = jnp.dot(tinv, (v32 * beta).astype(BF16),
                                      preferred_element_type=F32).astype(u_ref.dtype)
            w_ref[rows, sl] = jnp.dot(tinv, (kb * egc).astype(BF16),
                                      preferred_element_type=F32).astype(w_ref.dtype)
            qd_ref[rows, sl] = (q32 * egc).astype(qd_ref.dtype)
            kt_ref[rows, sl] = (k32 * jnp.exp(gates_ref[last, h:h + 1] - gc)).astype(kt_ref.dtype)
            pair.append(qk * _chunk_decay(gates_ref, gct_ref, c, rows, h, tril))
            if h % 2 == 1:
                attn_ref[rows, (h - 1) * B_CHUNK:(h + 1) * B_CHUNK] = jnp.concatenate(
                    pair, axis=-1).astype(attn_ref.dtype)
                pair = []
        return carry

    lax.fori_loop(0, B_STEP_CHUNKS, chunk, 0)


def _b_prep(qkv, gates, gct, tinv):
    t = qkv.shape[0]
    tm = B_STEP
    wide = lambda col: pl.BlockSpec((tm, B_WIDTH), lambda i, col=col: (i, col))
    narrow = pl.BlockSpec((tm, B_HEADS * B_CHUNK), lambda i: (i, 0))
    out_wide = jax.ShapeDtypeStruct((t, B_WIDTH), BF16)
    return pl.pallas_call(
        _b_prep_kernel, grid=(t // tm,),
        in_specs=[wide(0), wide(1), wide(2),
                  pl.BlockSpec((tm, 2 * B_HEADS), lambda i: (i, 0)),
                  pl.BlockSpec((B_STEP_CHUNKS, B_HEADS, B_CHUNK), lambda i: (i, 0, 0)),
                  pl.BlockSpec((B_PAIRS, tm, B_PAIR_W), lambda i: (0, i, 0))],
        out_specs=[wide(0)] * 4 + [narrow],
        out_shape=[out_wide] * 4 + [jax.ShapeDtypeStruct((t, B_HEADS * B_CHUNK), BF16)],
        compiler_params=_params(1), name="b_prep")(qkv, qkv, qkv, gates, gct, tinv)


def _b_scan_kernel(u_ref, w_ref, qd_ref, kt_ref, attn_ref, gates_ref, o_ref, state):
    @pl.when(pl.program_id(1) == 0)
    def _():
        state[...] = jnp.zeros_like(state)

    zero = jnp.zeros((B_CHUNK, B_HEAD_DIM), BF16)
    seqs = state.shape[0]
    heads = [(b, h) for b in range(seqs) for h in range(B_HEADS)]
    cols = lambda h: slice(h * B_HEAD_DIM, (h + 1) * B_HEAD_DIM)
    ws_qs = {}
    for b, h in heads:
        wq = jnp.concatenate([w_ref[b, :, cols(h)], qd_ref[b, :, cols(h)]], axis=0)
        ws_qs[b, h] = jnp.dot(wq, state[b, h].astype(BF16), preferred_element_type=F32)
    v_new = {}
    for b, h in heads:
        v_new[b, h] = (u_ref[b, :, cols(h)].astype(F32) - ws_qs[b, h][:B_CHUNK]).astype(BF16)
    for b, h in heads:
        chunk_decay = jnp.exp(gates_ref[b, B_CHUNK - 1:B_CHUNK, h:h + 1])
        state[b, h] = state[b, h] * chunk_decay + lax.dot_general(
            kt_ref[b, :, cols(h)], v_new[b, h], TN_DIMS, preferred_element_type=F32)
    for b in range(seqs):
        for hp in range(B_HEADS // 2):
            h0, h1 = 2 * hp, 2 * hp + 1
            vblock = jnp.concatenate(
                [jnp.concatenate([v_new[b, h0], zero], axis=1),
                 jnp.concatenate([zero, v_new[b, h1]], axis=1)], axis=0)
            intra = jnp.dot(attn_ref[b, :, h0 * B_CHUNK:(h1 + 1) * B_CHUNK], vblock,
                            preferred_element_type=F32)
            for j, h in enumerate((h0, h1)):
                o = ws_qs[b, h][B_CHUNK:] + intra[:, j * B_HEAD_DIM:(j + 1) * B_HEAD_DIM]
                o_ref[b, :, cols(h)] = o.astype(o_ref.dtype)


def _b_scan(u, w, qd, kt, attn, gates):
    t = u.shape[0]
    b = t // SEQ
    seqs = math.gcd(b, SCAN_SEQS)
    per_seq = lambda a: a.reshape(b, SEQ, a.shape[-1])
    spec = lambda width: pl.BlockSpec((seqs, B_CHUNK, width), lambda bi, n: (bi, n, 0))
    wide = spec(B_WIDTH)
    y = pl.pallas_call(
        _b_scan_kernel, grid=(b // seqs, N_CHUNKS),
        in_specs=[wide, wide, wide, wide, spec(B_HEADS * B_CHUNK), spec(2 * B_HEADS)],
        out_specs=wide,
        out_shape=jax.ShapeDtypeStruct((b, SEQ, B_WIDTH), BF16),
        scratch_shapes=[pltpu.VMEM((seqs, B_HEADS, B_HEAD_DIM, B_HEAD_DIM), F32)],
        compiler_params=pltpu.CompilerParams(dimension_semantics=("parallel", "arbitrary"),
                                             vmem_limit_bytes=VMEM_LIMIT),
        name="b_scan")(*(per_seq(a) for a in (u, w, qd, kt, attn, gates)))
    return y.reshape(t, B_WIDTH)


def kernel(x, norm_g, ffn_w_gate, ffn_w_up, ffn_w_down, rel_bias, a_w_in, a_w_out,
           b_w_in, b_conv_w, b_a_log, b_dt_bias, b_norm_w, b_w_out, final_g):
    batch, seq, d = x.shape
    assert (seq, d) == (SEQ, D_MODEL)
    depth = norm_g.shape[0]
    t = batch * seq
    x = x.reshape(t, d)
    bias = _a_bias_rows(rel_bias)
    norm_g = norm_g.astype(F32)
    for i in range(depth):
        j = i // 2
        x = _ffn(x, norm_g[i, 0][None], ffn_w_gate[i, 0].astype(BF16), ffn_w_up[i, 0].astype(BF16),
                 ffn_w_down[i, 0].astype(BF16))
        g_mix = norm_g[i, 1][None]
        if i % 2 == 0:
            proj = _a_in(x, g_mix, a_w_in[j].astype(BF16))
            o = _a_attn(proj.reshape(batch, seq, A_IN_WIDTH), bias)
            mixer, mixer_args = "a", (o.reshape(t, A_GROUP_WIDTH), a_w_out[j].astype(BF16))
        else:
            w_in = b_w_in[j].astype(BF16)
            qkv, z, gates, gct = _b_in(x, g_mix, w_in[:, :3 * B_WIDTH], w_in[:, 3 * B_WIDTH:4 * B_WIDTH],
                                       w_in[:, 4 * B_WIDTH:], b_conv_w[j].astype(F32),
                                       b_a_log[j], b_dt_bias[j])
            tinv = _b_inv(_b_low(qkv, gates, gct))
            u, w, qd, kt, attn = _b_prep(qkv, gates, gct, tinv)
            o = _b_scan(u, w, qd, kt, attn, gates)
            mixer, mixer_args = "b", (o, z, b_norm_w[j].astype(F32)[None], b_w_out[j].astype(BF16))
        last = i == depth - 1
        x = _ffn(x, norm_g[i, 2][None], ffn_w_gate[i, 1].astype(BF16), ffn_w_up[i, 1].astype(BF16),
                 ffn_w_down[i, 1].astype(BF16), final_g=final_g.astype(F32)[None] if last else None,
                 mixer=mixer, mixer_args=mixer_args)
    return x.reshape(batch, seq, d)
```

```python
import functools
import math

import numpy as np
import jax
import jax.numpy as jnp
from jax import lax
from jax.experimental import pallas as pl
from jax.experimental.pallas import tpu as pltpu

F32 = jnp.float32
BF16 = jnp.bfloat16

D_MODEL = 1024
SEQ = 2048
D_FF = 2816
NORM_EPS = 1e-6
MACARON_WEIGHT = 0.5

A_PATTERNS = ((128, 1), (512, 4), (2048, 16))
A_GROUPS = 3
A_HEADS = 8
A_HEAD_DIM = 64
A_GROUP_WIDTH = A_HEADS * A_HEAD_DIM
A_IN_WIDTH = A_GROUPS * 3 * A_GROUP_WIDTH
A_BLOCK = 128
A_PAIR = 2 * A_HEAD_DIM
A_UNROLL = 16
NEG_INF = -1e30
NUM_BUCKETS = 32
MAX_DISTANCE = 2048

B_HEADS = 8
B_HEAD_DIM = 128
B_WIDTH = B_HEADS * B_HEAD_DIM
B_CONV = 4
B_CHUNK = 64
N_CHUNKS = SEQ // B_CHUNK

VMEM_LIMIT = 56 * 1024 * 1024

NT_DIMS = (((1,), (1,)), ((), ()))
TN_DIMS = (((0,), (0,)), ((), ()))


def _const_spec(shape):
    zeros = (0,) * len(shape)
    return pl.BlockSpec(shape, lambda *_: zeros, pipeline_mode=pl.Buffered(1))


def _params(n_axes):
    return pltpu.CompilerParams(dimension_semantics=("parallel",) * n_axes,
                                vmem_limit_bytes=VMEM_LIMIT)


def _rms(x, g):
    ms = jnp.mean(x * x, axis=-1, keepdims=True)
    return x * lax.rsqrt(ms + NORM_EPS) * g


def _silu(x):
    return x * jax.nn.sigmoid(x)


def _softplus(x):
    return jnp.maximum(x, 0.0) + jnp.log1p(jnp.exp(-jnp.abs(x)))


def _gated_head_norm(o_ref, z_ref, nw):
    parts = []
    for h in range(B_HEADS):
        sl = slice(h * B_HEAD_DIM, (h + 1) * B_HEAD_DIM)
        y = _rms(o_ref[:, sl].astype(F32), nw) * _silu(z_ref[:, sl].astype(F32))
        parts.append(y.astype(BF16))
    return jnp.concatenate(parts, axis=1)


def _ffn_kernel(x_ref, *refs, mixer, final):
    refs = list(refs)
    o_ref = refs.pop()
    x = x_ref[...]
    if mixer == "a":
        y_ref, wo_ref = refs[:2]
        refs = refs[2:]
        x = x + jnp.dot(y_ref[...], wo_ref[...], preferred_element_type=F32)
    elif mixer == "b":
        y_ref, z_ref, nw_ref, wo_ref = refs[:4]
        refs = refs[4:]
        y = _gated_head_norm(y_ref, z_ref, nw_ref[...])
        x = x + jnp.dot(y, wo_ref[...], preferred_element_type=F32)
    g_ref, wg_ref, wu_ref, wd_ref = refs[:4]
    h = _rms(x, g_ref[...]).astype(BF16)
    gate = jnp.dot(h, wg_ref[...], preferred_element_type=F32)
    up = jnp.dot(h, wu_ref[...], preferred_element_type=F32)
    a = (_silu(gate) * up).astype(BF16)
    y = x + MACARON_WEIGHT * jnp.dot(a, wd_ref[...], preferred_element_type=F32)
    if final:
        y = _rms(y, refs[4][...])
    o_ref[...] = y


def _ffn(x, g, wg, wu, wd, final_g=None, mixer=None, mixer_args=(), tm=512):
    t = x.shape[0]
    row = pl.BlockSpec((tm, D_MODEL), lambda i: (i, 0))
    in_specs, args = [row], [x]
    for a in mixer_args:
        if a.shape[0] == t:
            in_specs.append(pl.BlockSpec((tm, a.shape[1]), lambda i: (i, 0)))
        else:
            in_specs.append(_const_spec(a.shape))
        args.append(a)
    in_specs += [_const_spec((1, D_MODEL)), _const_spec((D_MODEL, D_FF)),
                 _const_spec((D_MODEL, D_FF)), _const_spec((D_FF, D_MODEL))]
    args += [g, wg, wu, wd]
    if final_g is not None:
        in_specs.append(_const_spec((1, D_MODEL)))
        args.append(final_g)
    return pl.pallas_call(
        functools.partial(_ffn_kernel, mixer=mixer, final=final_g is not None),
        grid=(t // tm,), in_specs=in_specs, out_specs=row,
        out_shape=jax.ShapeDtypeStruct((t, D_MODEL), F32),
        compiler_params=_params(1), name="ffn")(*args)


def _a_in_kernel(x_ref, g_ref, w_ref, o_ref):
    h = _rms(x_ref[...], g_ref[...]).astype(BF16)
    o_ref[...] = jnp.dot(h, w_ref[...], preferred_element_type=F32)


def _a_in(x, g, w, tm=512):
    t = x.shape[0]
    return pl.pallas_call(
        _a_in_kernel, grid=(t // tm,),
        in_specs=[pl.BlockSpec((tm, D_MODEL), lambda i: (i, 0)),
                  _const_spec((1, D_MODEL)), _const_spec((D_MODEL, A_IN_WIDTH))],
        out_specs=pl.BlockSpec((tm, A_IN_WIDTH), lambda i: (i, 0)),
        out_shape=jax.ShapeDtypeStruct((t, A_IN_WIDTH), F32),
        compiler_params=_params(1), name="a_in")(x, g, w)


def _t5_bucket_np(distance):
    max_exact = NUM_BUCKETS // 2
    n = distance.astype(np.float32)
    large = np.float32(max_exact) + (
        np.log(np.maximum(n, np.float32(1.0)) / np.float32(max_exact))
        / np.float32(math.log(MAX_DISTANCE / max_exact)) * np.float32(NUM_BUCKETS - max_exact))
    large = np.minimum(large.astype(np.int32), NUM_BUCKETS - 1)
    return np.where(distance < max_exact, distance, large).astype(np.int32)


def _a_bias_rows(rel_bias):
    m = np.arange(2 * A_BLOCK)
    back = A_BLOCK - m
    rows = []
    for g, (window, dilation) in enumerate(A_PATTERNS):
        assert window // dilation == A_BLOCK
        bucket = _t5_bucket_np(np.maximum(back, 0) * dilation)
        b = rel_bias.astype(F32)[:, g * A_HEADS:(g + 1) * A_HEADS][bucket]
        rows.append(jnp.where((back >= 0)[:, None], b, NEG_INF).T)
    return jnp.stack(rows)


def _a_units(q_ref, k_ref, v_ref, bias_scr, g, o_ref, l_ref, units, stride):
    def rows(start, n):
        return pl.ds(start, n) if stride == 1 else pl.ds(start, n, stride=stride)

    head0 = lax.broadcasted_iota(jnp.int32, (1, A_PAIR), 1) < A_HEAD_DIM
    mine = (head0, jnp.logical_not(head0))
    values, scores = [], []
    for base, first in units:
        nk = A_BLOCK if first else 2 * A_BLOCK
        k0 = base if first else base - A_BLOCK * stride
        q = q_ref[rows(base, A_BLOCK), :] * (A_HEAD_DIM ** -0.5)
        k = k_ref[rows(k0, nk), :].astype(BF16)
        values.append(v_ref[rows(k0, nk), :])
        for h in range(2):
            qh = jnp.where(mine[h], q, 0.0).astype(BF16)
            s = lax.dot_general(qh, k, NT_DIMS, preferred_element_type=F32)
            scores.append(s + (bias_scr[g, h, :, A_BLOCK:] if first else bias_scr[g, h]))
    maxes = [jnp.max(s, axis=-1, keepdims=True) for s in scores]
    probs = [jnp.exp(s - m).astype(BF16) for s, m in zip(scores, maxes)]
    results = []
    for n, p in enumerate(probs):
        ve = jnp.where(mine[n % 2], values[n // 2], 1.0).astype(BF16)
        results.append(jnp.dot(p, ve, preferred_element_type=F32))
    for u, (base, _) in enumerate(units):
        r0, r1 = results[2 * u], results[2 * u + 1]
        den = pltpu.roll(jnp.where(head0, r1, r0), A_HEAD_DIM, 1)
        o_ref[rows(base, A_BLOCK), :] = jnp.where(head0, r0, r1) / den
        l_ref[rows(base, A_BLOCK), :] = (jnp.where(head0, maxes[2 * u], maxes[2 * u + 1])
                                         + jnp.log(den))


def _a_attn_kernel(q0, k0, v0, q1, k1, v1, q2, k2, v2, brow_ref, o_ref,
                   bias_scr, sub_scr, o_scr, l_scr, tok_scr):
    pair = pl.program_id(1)
    for g in range(A_GROUPS):
        for h in range(2):
            row = brow_ref[g, pl.ds(2 * pair + h, 1), :]
            bias_scr[g, h] = pltpu.roll(jnp.broadcast_to(row, (A_BLOCK, 2 * A_BLOCK)), 0, 1,
                                        stride=1, stride_axis=0)

    n_blocks = SEQ // A_BLOCK
    quarter = SEQ // 4

    units0 = functools.partial(_a_units, q0, k0, v0, bias_scr, 0, o_scr.at[0], l_scr.at[0], stride=1)
    units0([(u * A_BLOCK, u == 0) for u in range(A_UNROLL)])

    def later_blocks(ib, carry):
        units0([(pl.multiple_of((ib * A_UNROLL + u) * A_BLOCK, A_BLOCK), False)
                for u in range(A_UNROLL)])
        return carry

    if n_blocks > A_UNROLL:
        lax.fori_loop(1, n_blocks // A_UNROLL, later_blocks, 0)

    units1 = functools.partial(_a_units, q1, k1, v1, bias_scr, 1, o_scr.at[1], l_scr.at[1], stride=4)

    per_sub = n_blocks // 4
    subs = max(A_UNROLL // per_sub, 1)

    def subsequence(cb, carry):
        units1([(cb * subs + s + 4 * A_BLOCK * i, i == 0) for s in range(subs) for i in range(per_sub)])
        return carry

    lax.fori_loop(0, 4 // subs, subsequence, 0)

    for r, src in enumerate((q2, k2, v2)):
        for b in range(4):
            sub_scr[r, pl.ds(b * quarter, quarter), :] = src[pl.ds(b, quarter, stride=4), :]
    units2 = functools.partial(_a_units, sub_scr.at[0], sub_scr.at[1], sub_scr.at[2], bias_scr, 2,
                               o_scr.at[2], l_scr.at[2], stride=4)

    rows_per_body = A_UNROLL // 4

    def subsequences(ab, carry):
        units2([(b * quarter + ab * rows_per_body + a, True) for a in range(rows_per_body) for b in range(4)])
        return carry

    lax.fori_loop(0, 4 // rows_per_body, subsequences, 0)
    for n, scr in enumerate((o_scr, l_scr)):
        for b in range(4):
            tok_scr[n, pl.ds(b, quarter, stride=4), :] = scr[2, pl.ds(b * quarter, quarter), :]

    rows_per_step = 256

    def merge(r, carry):
        sl = pl.ds(pl.multiple_of(r * rows_per_step, rows_per_step), rows_per_step)
        l0, l1, l2 = l_scr[0, sl, :], l_scr[1, sl, :], tok_scr[1, sl, :]
        m = jnp.maximum(jnp.maximum(l0, l1), l2)
        w0, w1, w2 = jnp.exp(l0 - m), jnp.exp(l1 - m), jnp.exp(l2 - m)
        num = w0 * o_scr[0, sl, :] + w1 * o_scr[1, sl, :] + w2 * tok_scr[0, sl, :]
        o_ref[sl, :] = (num / (w0 + w1 + w2)).astype(o_ref.dtype)
        return carry

    lax.fori_loop(0, SEQ // rows_per_step, merge, 0)


def _a_attn(proj, bias_rows):
    b = proj.shape[0]
    n_pairs = A_GROUP_WIDTH // A_PAIR
    in_specs = []
    for g in range(A_GROUPS):
        for r in range(3):
            col = (g * 3 + r) * n_pairs
            in_specs.append(pl.BlockSpec((None, SEQ, A_PAIR),
                                         lambda bi, hp, col=col: (bi, 0, col + hp)))
    in_specs.append(_const_spec((A_GROUPS, A_HEADS, 2 * A_BLOCK)))
    return pl.pallas_call(
        _a_attn_kernel, grid=(b, n_pairs), in_specs=in_specs,
        out_specs=pl.BlockSpec((None, SEQ, A_PAIR), lambda bi, hp: (bi, 0, hp)),
        out_shape=jax.ShapeDtypeStruct((b, SEQ, A_GROUP_WIDTH), BF16),
        scratch_shapes=[pltpu.VMEM((A_GROUPS, 2, A_BLOCK, 2 * A_BLOCK), F32),
                        pltpu.VMEM((3, SEQ, A_PAIR), F32),
                        pltpu.VMEM((A_GROUPS, SEQ, A_PAIR), F32),
                        pltpu.VMEM((A_GROUPS, SEQ, A_PAIR), F32),
                        pltpu.VMEM((2, SEQ, A_PAIR), F32)],
        compiler_params=_params(2), name="a_attn")(*([proj] * 9), bias_rows)


B_IN_TILE = 512
B_PHASES = 8
B_PH_ROWS = B_IN_TILE // B_PHASES
B_PAD = 8
B_COL_BLOCK = 512


def _b_in_kernel(x_ref, g_ref, wqkv_ref, wz_ref, wab_ref, conv_ref, alog_ref, dtb_ref,
                 qkv_ref, z_ref, gates_ref, pre_scr):
    tiles_per_seq = SEQ // B_IN_TILE
    qkv_w = 3 * B_WIDTH
    gate_w = 2 * B_HEADS
    slab = lambda a, width: slice(a * width, (a + 1) * width)
    rows = pl.ds(B_PAD, B_PH_ROWS)
    carry_row = pl.ds(B_PAD - 1, 1)
    wrapped = range(B_PHASES - (B_CONV - 1), B_PHASES)

    x = jnp.concatenate([x_ref[:, slab(a, D_MODEL)] for a in range(B_PHASES)], axis=0)
    h = _rms(x, g_ref[...]).astype(BF16)

    @pl.when(pl.program_id(0) % tiles_per_seq == 0)
    def _():
        for n in range(len(wrapped)):
            pre_scr[n, carry_row, :] = jnp.zeros((1, qkv_w), F32)

    ab = jnp.dot(h, wab_ref[...], preferred_element_type=F32)
    glog = -jnp.exp(alog_ref[...]) * _softplus(ab + dtb_ref[...])
    beta = jax.nn.sigmoid(ab)
    over_phases = [glog[:B_PH_ROWS]]
    for a in range(1, B_PHASES):
        over_phases.append(over_phases[-1] + glog[a * B_PH_ROWS:(a + 1) * B_PH_ROWS])
    row_total = over_phases[-1]
    pos = lax.broadcasted_iota(jnp.int32, row_total.shape, 0) & (B_CHUNK // B_PHASES - 1)
    incl = row_total
    step = 1
    while step < B_CHUNK // B_PHASES:
        incl = incl + jnp.where(pos >= step, pltpu.roll(incl, step, 0), 0.0)
        step *= 2
    earlier_rows = incl - row_total
    is_decay = lax.broadcasted_iota(jnp.int32, (1, gate_w), 1) < B_HEADS
    gates_ref[...] = jnp.concatenate(
        [jnp.where(is_decay, over_phases[a] + earlier_rows, beta[a * B_PH_ROWS:(a + 1) * B_PH_ROWS])
         for a in range(B_PHASES)], axis=1)

    def gate_block(jz):
        cols = slice(jz * B_COL_BLOCK, (jz + 1) * B_COL_BLOCK)
        zz = jnp.dot(h, wz_ref[:, cols], preferred_element_type=F32)
        for a in range(B_PHASES):
            z_ref[:, a * B_WIDTH + jz * B_COL_BLOCK:a * B_WIDTH + (jz + 1) * B_COL_BLOCK] = (
                zz[a * B_PH_ROWS:(a + 1) * B_PH_ROWS].astype(z_ref.dtype))

    def qkv_block(jb):
        cols = slice(jb * B_COL_BLOCK, (jb + 1) * B_COL_BLOCK)
        pre = jnp.dot(h, wqkv_ref[:, cols], preferred_element_type=F32)
        phase = lambda a: pre[a * B_PH_ROWS:(a + 1) * B_PH_ROWS]
        up_one = {}
        for n, b in enumerate(wrapped):
            pre_scr[n, rows, cols] = phase(b)
            up_one[b] = pre_scr[n, pl.ds(B_PAD - 1, B_PH_ROWS), cols]
        for a in range(B_PHASES):
            conv = conv_ref[B_CONV - 1:B_CONV, cols] * phase(a)
            for s in range(1, B_CONV):
                src = phase(a - s) if a >= s else up_one[a - s + B_PHASES]
                conv = conv + conv_ref[B_CONV - 1 - s:B_CONV - s, cols] * src
            act = _silu(conv)
            base = a * qkv_w + jb * B_COL_BLOCK
            for hd in range(B_COL_BLOCK // B_HEAD_DIM):
                t = act[:, hd * B_HEAD_DIM:(hd + 1) * B_HEAD_DIM]
                if jb * B_COL_BLOCK < 2 * B_WIDTH:
                    t = t * lax.rsqrt(jnp.sum(t * t, axis=-1, keepdims=True) + NORM_EPS)
                if jb * B_COL_BLOCK < B_WIDTH:
                    t = t * (B_HEAD_DIM ** -0.5)
                qkv_ref[:, base + hd * B_HEAD_DIM:base + (hd + 1) * B_HEAD_DIM] = t.astype(qkv_ref.dtype)
        for n in range(len(wrapped)):
            pre_scr[n, carry_row, cols] = pre_scr[n, pl.ds(B_PAD + B_PH_ROWS - 1, 1), cols]

    n_qkv, n_z = qkv_w // B_COL_BLOCK, B_WIDTH // B_COL_BLOCK
    for jb in range(n_qkv):
        qkv_block(jb)
        if (jb + 1) % (n_qkv // n_z) == 0:
            gate_block((jb + 1) // (n_qkv // n_z) - 1)


def _b_in(x, g, wqkv, wz, wab, conv_w, a_log, dt_bias):
    t = x.shape[0]
    rows = t // B_PHASES
    pad = jnp.zeros((1, B_HEADS), F32)
    a_log = jnp.concatenate([a_log.reshape(1, B_HEADS).astype(F32), pad], axis=1)
    dt_bias = jnp.concatenate([dt_bias.reshape(1, B_HEADS).astype(F32), pad], axis=1)
    widths = (3 * B_WIDTH, B_WIDTH, 2 * B_HEADS)
    tile = lambda width: pl.BlockSpec((B_PH_ROWS, B_PHASES * width), lambda i: (i, 0))
    qkv, z, gates = pl.pallas_call(
        _b_in_kernel, grid=(t // B_IN_TILE,),
        in_specs=[tile(D_MODEL),
                  _const_spec((1, D_MODEL)), _const_spec((D_MODEL, 3 * B_WIDTH)),
                  _const_spec((D_MODEL, B_WIDTH)), _const_spec((D_MODEL, 2 * B_HEADS)),
                  _const_spec((B_CONV, 3 * B_WIDTH)),
                  _const_spec((1, 2 * B_HEADS)), _const_spec((1, 2 * B_HEADS))],
        out_specs=[tile(w) for w in widths],
        out_shape=[jax.ShapeDtypeStruct((rows, B_PHASES * w), dt)
                   for w, dt in zip(widths, (BF16, BF16, F32))],
        scratch_shapes=[pltpu.VMEM((B_CONV - 1, B_PAD + B_PH_ROWS, 3 * B_WIDTH), F32)],
        compiler_params=pltpu.CompilerParams(dimension_semantics=("arbitrary",),
                                             vmem_limit_bytes=VMEM_LIMIT),
        name="b_in")(x.reshape(rows, B_PHASES * D_MODEL), g, wqkv, wz, wab, conv_w, a_log, dt_bias)
    return qkv.reshape(t, widths[0]), z.reshape(t, widths[1]), gates.reshape(t, widths[2])


B_STEP_CHUNKS = 8
B_STEP = B_STEP_CHUNKS * B_CHUNK
B_PAIRS = B_HEADS // 2
B_PAIR_W = 2 * B_CHUNK
B_CHUNK_UNROLL = 2
INV_ROWS = 8
XPOSE_ROWS = 4
SCAN_SEQS = 4


def _chunk_rows(c):
    return pl.ds(c, B_CHUNK, stride=B_STEP_CHUNKS)


def _gates_by_head(gates_ref, rows):
    tile = gates_ref[rows, :]
    lanes = 128
    padded = jnp.concatenate([tile, jnp.zeros((B_CHUNK, lanes - tile.shape[1]), F32)], axis=1)
    return padded.T[:tile.shape[1]]


def _chunk_decay(gates_ref, by_head, rows, h, keep):
    gc = gates_ref[rows, h:h + 1]
    return jnp.where(keep, jnp.exp(jnp.minimum(gc - by_head[h:h + 1, :], 0.0)), 0.0)


def _b_low_kernel(k_ref, gates_ref, l_ref, byh_ref):
    upper = (lax.broadcasted_iota(jnp.int32, (B_CHUNK, B_CHUNK), 0)
             < lax.broadcasted_iota(jnp.int32, (B_CHUNK, B_CHUNK), 1))

    def chunk(c, carry):
        rows = pl.ds(pl.multiple_of(c * B_CHUNK, B_CHUNK), B_CHUNK)
        by_head = _gates_by_head(gates_ref, rows)
        byh_ref[c] = by_head
        pair = []
        for h in range(B_HEADS):
            k = k_ref[rows, h * B_HEAD_DIM:(h + 1) * B_HEAD_DIM]
            kk = lax.dot_general(k, k, NT_DIMS, preferred_element_type=F32)
            gc_j = gates_ref[rows, h:h + 1]
            decay = jnp.where(upper, jnp.exp(jnp.minimum(by_head[h:h + 1, :] - gc_j, 0.0)), 0.0)
            pair.append(kk * by_head[B_HEADS + h:B_HEADS + h + 1, :] * decay)
            if h % 2 == 1:
                l_ref[h // 2, _chunk_rows(c), :] = jnp.concatenate(pair, axis=-1)
                pair = []
        return carry

    def chunks(cb, carry):
        for n in range(B_CHUNK_UNROLL):
            chunk(cb * B_CHUNK_UNROLL + n, carry)
        return carry

    lax.fori_loop(0, B_STEP_CHUNKS // B_CHUNK_UNROLL, chunks, 0)


def _b_low(qkv, gates):
    t = qkv.shape[0]
    return pl.pallas_call(
        _b_low_kernel, grid=(t // B_STEP,),
        in_specs=[pl.BlockSpec((B_STEP, B_WIDTH), lambda i: (i, 1)),
                  pl.BlockSpec((B_STEP, 2 * B_HEADS), lambda i: (i, 0))],
        out_specs=[pl.BlockSpec((B_PAIRS, B_STEP, B_PAIR_W), lambda i: (0, i, 0)),
                   pl.BlockSpec((B_STEP_CHUNKS, 2 * B_HEADS, B_CHUNK), lambda i: (i, 0, 0))],
        out_shape=[jax.ShapeDtypeStruct((B_PAIRS, t, B_PAIR_W), F32),
                   jax.ShapeDtypeStruct((t // B_CHUNK, 2 * B_HEADS, B_CHUNK), F32)],
        compiler_params=_params(1), name="b_low")(qkv, gates)


def _b_inv_kernel(l_ref, t_ref, lt_scr, xt_scr):
    steps = SEQ // B_STEP

    def tiles(i):
        return [(pr, pl.ds(pl.multiple_of(s * B_STEP + i * B_STEP_CHUNKS, B_STEP_CHUNKS), B_STEP_CHUNKS))
                for pr in range(B_PAIRS) for s in range(steps)]

    def to_problem_lanes(ib, carry):
        for i in [ib * XPOSE_ROWS + r for r in range(XPOSE_ROWS)]:
            slab_t = jnp.concatenate([l_ref[pr, rows, :] for pr, rows in tiles(i)], axis=0).T
            lt_scr[0, i] = slab_t[:B_CHUNK]
            lt_scr[1, i] = slab_t[B_CHUNK:]
        return carry

    lax.fori_loop(0, B_CHUNK // XPOSE_ROWS, to_problem_lanes, 0)

    xt_scr[...] = jnp.zeros_like(xt_scr)
    sub = lax.broadcasted_iota(jnp.int32, (INV_ROWS, N_CHUNKS * B_PAIRS), 0)
    groups = B_CHUNK // INV_ROWS

    def column_block(hc, carry):
        half, cb = hc // groups, hc % groups
        cols = pl.ds(pl.multiple_of(cb * INV_ROWS, INV_ROWS), INV_ROWS)

        def row_group(ib, carry2):
            i0 = ib * INV_ROWS
            accs = tuple(jnp.where(jnp.logical_and(ib == cb, sub == r), 1.0, 0.0)
                         for r in range(INV_ROWS))

            def earlier_rows(jb, accs):
                for jj in range(INV_ROWS):
                    j = jb * INV_ROWS + jj
                    xj = xt_scr[half, j, cols, :]
                    accs = tuple(accs[r] - lt_scr[half, j, pl.ds(i0 + r, 1), :] * xj
                                 for r in range(INV_ROWS))
                return accs

            accs = list(lax.fori_loop(cb, ib, earlier_rows, accs))
            for r in range(INV_ROWS):
                for r2 in range(r):
                    accs[r] = accs[r] - lt_scr[half, i0 + r2, pl.ds(i0 + r, 1), :] * accs[r2]
                xt_scr[half, i0 + r, cols, :] = accs[r]
            return carry2

        lax.fori_loop(cb, groups, row_group, 0)
        return carry

    lax.fori_loop(0, 2 * groups, column_block, 0)

    def from_problem_lanes(ib, carry):
        for i in [ib * XPOSE_ROWS + r for r in range(XPOSE_ROWS)]:
            slab_t = jnp.concatenate([xt_scr[0, i], xt_scr[1, i]], axis=0).T
            for n, (pr, rows) in enumerate(tiles(i)):
                t_ref[pr, rows, :] = slab_t[n * B_STEP_CHUNKS:(n + 1) * B_STEP_CHUNKS]
        return carry

    lax.fori_loop(0, B_CHUNK // XPOSE_ROWS, from_problem_lanes, 0)


def _b_inv(low):
    t = low.shape[1]
    block = pl.BlockSpec((B_PAIRS, SEQ, B_PAIR_W), lambda b: (0, b, 0))
    scr = pltpu.VMEM((2, B_CHUNK, B_CHUNK, N_CHUNKS * B_PAIRS), F32)
    return pl.pallas_call(
        _b_inv_kernel, grid=(t // SEQ,), in_specs=[block], out_specs=block,
        out_shape=jax.ShapeDtypeStruct(low.shape, F32), scratch_shapes=[scr, scr],
        compiler_params=_params(1), name="b_inv")(low)


def _b_prep_kernel(q_ref, k_ref, v_ref, gates_ref, byh_ref, t_ref,
                   u_ref, w_ref, qd_ref, kt_ref, attn_ref):
    tril = (lax.broadcasted_iota(jnp.int32, (B_CHUNK, B_CHUNK), 0)
            >= lax.broadcasted_iota(jnp.int32, (B_CHUNK, B_CHUNK), 1))

    def chunk(c, carry):
        rows = pl.ds(pl.multiple_of(c * B_CHUNK, B_CHUNK), B_CHUNK)
        last = pl.ds(c * B_CHUNK + B_CHUNK - 1, 1)
        by_head = byh_ref[c]
        pair = []
        for h in range(B_HEADS):
            sl = slice(h * B_HEAD_DIM, (h + 1) * B_HEAD_DIM)
            q, k, v = q_ref[rows, sl], k_ref[rows, sl], v_ref[rows, sl]
            q32, k32, v32 = q.astype(F32), k.astype(F32), v.astype(F32)
            gc = gates_ref[rows, h:h + 1]
            beta = gates_ref[rows, B_HEADS + h:B_HEADS + h + 1]
            egc = jnp.exp(gc)
            kb = k32 * beta
            qk = lax.dot_general(q, k, NT_DIMS, preferred_element_type=F32)
            if h % 2 == 0:
                tinv_pair = t_ref[h // 2, _chunk_rows(c), :]
            tinv = tinv_pair[:, (h % 2) * B_CHUNK:(h % 2 + 1) * B_CHUNK].astype(BF16)
            u_ref[rows, sl] = jnp.dot(tinv, (v32 * beta).astype(BF16),
                                      preferred_element_type=F32).astype(u_ref.dtype)
            w_ref[rows, sl] = jnp.dot(tinv, (kb * egc).astype(BF16),
                                      preferred_element_type=F32).astype(w_ref.dtype)
            qd_ref[rows, sl] = (q32 * egc).astype(qd_ref.dtype)
            kt_ref[rows, sl] = (k32 * jnp.exp(gates_ref[last, h:h + 1] - gc)).astype(kt_ref.dtype)
            pair.append(qk * _chunk_decay(gates_ref, by_head, rows, h, tril))
            if h % 2 == 1:
                attn_ref[rows, (h - 1) * B_CHUNK:(h + 1) * B_CHUNK] = jnp.concatenate(
                    pair, axis=-1).astype(attn_ref.dtype)
                pair = []
        return carry

    lax.fori_loop(0, B_STEP_CHUNKS, chunk, 0)


def _b_prep(qkv, gates, by_head, tinv):
    t = qkv.shape[0]
    tm = B_STEP
    wide = lambda col: pl.BlockSpec((tm, B_WIDTH), lambda i, col=col: (i, col))
    narrow = pl.BlockSpec((tm, B_HEADS * B_CHUNK), lambda i: (i, 0))
    out_wide = jax.ShapeDtypeStruct((t, B_WIDTH), BF16)
    return pl.pallas_call(
        _b_prep_kernel, grid=(t // tm,),
        in_specs=[wide(0), wide(1), wide(2),
                  pl.BlockSpec((tm, 2 * B_HEADS), lambda i: (i, 0)),
                  pl.BlockSpec((B_STEP_CHUNKS, 2 * B_HEADS, B_CHUNK), lambda i: (i, 0, 0)),
                  pl.BlockSpec((B_PAIRS, tm, B_PAIR_W), lambda i: (0, i, 0))],
        out_specs=[wide(0)] * 4 + [narrow],
        out_shape=[out_wide] * 4 + [jax.ShapeDtypeStruct((t, B_HEADS * B_CHUNK), BF16)],
        compiler_params=_params(1), name="b_prep")(qkv, qkv, qkv, gates, by_head, tinv)


def _b_scan_kernel(u_ref, w_ref, qd_ref, kt_ref, attn_ref, gates_ref, o_ref, state):
    @pl.when(pl.program_id(1) == 0)
    def _():
        state[...] = jnp.zeros_like(state)

    zero = jnp.zeros((B_CHUNK, B_HEAD_DIM), BF16)
    seqs = state.shape[0]
    heads = [(b, h) for b in range(seqs) for h in range(B_HEADS)]
    cols = lambda h: slice(h * B_HEAD_DIM, (h + 1) * B_HEAD_DIM)
    ws_qs = {}
    for b, h in heads:
        wq = jnp.concatenate([w_ref[b, :, cols(h)], qd_ref[b, :, cols(h)]], axis=0)
        ws_qs[b, h] = jnp.dot(wq, state[b, h].astype(BF16), preferred_element_type=F32)
    v_new = {}
    for b, h in heads:
        v_new[b, h] = (u_ref[b, :, cols(h)].astype(F32) - ws_qs[b, h][:B_CHUNK]).astype(BF16)
    for b, h in heads:
        chunk_decay = jnp.exp(gates_ref[b, B_CHUNK - 1:B_CHUNK, h:h + 1])
        state[b, h] = state[b, h] * chunk_decay + lax.dot_general(
            kt_ref[b, :, cols(h)], v_new[b, h], TN_DIMS, preferred_element_type=F32)
    for b in range(seqs):
        for hp in range(B_HEADS // 2):
            h0, h1 = 2 * hp, 2 * hp + 1
            vblock = jnp.concatenate(
                [jnp.concatenate([v_new[b, h0], zero], axis=1),
                 jnp.concatenate([zero, v_new[b, h1]], axis=1)], axis=0)
            intra = jnp.dot(attn_ref[b, :, h0 * B_CHUNK:(h1 + 1) * B_CHUNK], vblock,
                            preferred_element_type=F32)
            for j, h in enumerate((h0, h1)):
                o = ws_qs[b, h][B_CHUNK:] + intra[:, j * B_HEAD_DIM:(j + 1) * B_HEAD_DIM]
                o_ref[b, :, cols(h)] = o.astype(o_ref.dtype)


def _b_scan(u, w, qd, kt, attn, gates):
    t = u.shape[0]
    b = t // SEQ
    seqs = math.gcd(b, SCAN_SEQS)
    per_seq = lambda a: a.reshape(b, SEQ, a.shape[-1])
    spec = lambda width: pl.BlockSpec((seqs, B_CHUNK, width), lambda bi, n: (bi, n, 0))
    wide = spec(B_WIDTH)
    y = pl.pallas_call(
        _b_scan_kernel, grid=(b // seqs, N_CHUNKS),
        in_specs=[wide, wide, wide, wide, spec(B_HEADS * B_CHUNK), spec(2 * B_HEADS)],
        out_specs=wide,
        out_shape=jax.ShapeDtypeStruct((b, SEQ, B_WIDTH), BF16),
        scratch_shapes=[pltpu.VMEM((seqs, B_HEADS, B_HEAD_DIM, B_HEAD_DIM), F32)],
        compiler_params=pltpu.CompilerParams(dimension_semantics=("parallel", "arbitrary"),
                                             vmem_limit_bytes=VMEM_LIMIT),
        name="b_scan")(*(per_seq(a) for a in (u, w, qd, kt, attn, gates)))
    return y.reshape(t, B_WIDTH)


def kernel(x, norm_g, ffn_w_gate, ffn_w_up, ffn_w_down, rel_bias, a_w_in, a_w_out,
           b_w_in, b_conv_w, b_a_log, b_dt_bias, b_norm_w, b_w_out, final_g):
    batch, seq, d = x.shape
    assert (seq, d) == (SEQ, D_MODEL)
    depth = norm_g.shape[0]
    t = batch * seq
    x = x.reshape(t, d)
    bias = _a_bias_rows(rel_bias)
    norm_g = norm_g.astype(F32)
    for i in range(depth):
        j = i // 2
        x = _ffn(x, norm_g[i, 0][None], ffn_w_gate[i, 0].astype(BF16), ffn_w_up[i, 0].astype(BF16),
                 ffn_w_down[i, 0].astype(BF16))
        g_mix = norm_g[i, 1][None]
        if i % 2 == 0:
            proj = _a_in(x, g_mix, a_w_in[j].astype(BF16))
            o = _a_attn(proj.reshape(batch, seq, A_IN_WIDTH), bias)
            mixer, mixer_args = "a", (o.reshape(t, A_GROUP_WIDTH), a_w_out[j].astype(BF16))
        else:
            w_in = b_w_in[j].astype(BF16)
            qkv, z, gates = _b_in(x, g_mix, w_in[:, :3 * B_WIDTH], w_in[:, 3 * B_WIDTH:4 * B_WIDTH],
                                  w_in[:, 4 * B_WIDTH:], b_conv_w[j].astype(F32),
                                  b_a_log[j], b_dt_bias[j])
            low, by_head = _b_low(qkv, gates)
            u, w, qd, kt, attn = _b_prep(qkv, gates, by_head, _b_inv(low))
            o = _b_scan(u, w, qd, kt, attn, gates)
            mixer, mixer_args = "b", (o, z, b_norm_w[j].astype(F32)[None], b_w_out[j].astype(BF16))
        last = i == depth - 1
        x = _ffn(x, norm_g[i, 2][None], ffn_w_gate[i, 1].astype(BF16), ffn_w_up[i, 1].astype(BF16),
                 ffn_w_down[i, 1].astype(BF16), final_g=final_g.astype(F32)[None] if last else None,
                 mixer=mixer, mixer_args=mixer_args)
    return x.reshape(batch, seq, d)
```

```python
import functools
import math

import numpy as np
import jax
import jax.numpy as jnp
from jax import lax
from jax.experimental import pallas as pl
from jax.experimental.pallas import tpu as pltpu

F32 = jnp.float32
BF16 = jnp.bfloat16

D_MODEL = 1024
SEQ = 2048
D_FF = 2816
NORM_EPS = 1e-6
MACARON_WEIGHT = 0.5

A_PATTERNS = ((128, 1), (512, 4), (2048, 16))
A_GROUPS = 3
A_HEADS = 8
A_HEAD_DIM = 64
A_GROUP_WIDTH = A_HEADS * A_HEAD_DIM
A_IN_WIDTH = A_GROUPS * 3 * A_GROUP_WIDTH
A_BLOCK = 128
A_PAIR = 2 * A_HEAD_DIM
A_UNROLL = 16
NEG_INF = -1e30
NUM_BUCKETS = 32
MAX_DISTANCE = 2048

B_HEADS = 8
B_HEAD_DIM = 128
B_WIDTH = B_HEADS * B_HEAD_DIM
B_CONV = 4
B_CHUNK = 64
N_CHUNKS = SEQ // B_CHUNK

VMEM_LIMIT = 56 * 1024 * 1024

NT_DIMS = (((1,), (1,)), ((), ()))
TN_DIMS = (((0,), (0,)), ((), ()))


def _const_spec(shape):
    zeros = (0,) * len(shape)
    return pl.BlockSpec(shape, lambda *_: zeros, pipeline_mode=pl.Buffered(1))


def _params(n_axes):
    return pltpu.CompilerParams(dimension_semantics=("parallel",) * n_axes,
                                vmem_limit_bytes=VMEM_LIMIT)


def _rms(x, g):
    ms = jnp.mean(x * x, axis=-1, keepdims=True)
    return x * lax.rsqrt(ms + NORM_EPS) * g


def _silu(x):
    return x * jax.nn.sigmoid(x)


def _softplus(x):
    return jnp.maximum(x, 0.0) + jnp.log1p(jnp.exp(-jnp.abs(x)))


def _gated_head_norm(o_ref, z_ref, nw):
    parts = []
    for h in range(B_HEADS):
        sl = slice(h * B_HEAD_DIM, (h + 1) * B_HEAD_DIM)
        y = _rms(o_ref[:, sl].astype(F32), nw) * _silu(z_ref[:, sl].astype(F32))
        parts.append(y.astype(BF16))
    return jnp.concatenate(parts, axis=1)


def _ffn_kernel(x_ref, *refs, mixer, final):
    refs = list(refs)
    o_ref = refs.pop()
    x = x_ref[...]
    if mixer == "a":
        y_ref, wo_ref = refs[:2]
        refs = refs[2:]
        x = x + jnp.dot(y_ref[...], wo_ref[...], preferred_element_type=F32)
    elif mixer == "b":
        y_ref, z_ref, nw_ref, wo_ref = refs[:4]
        refs = refs[4:]
        y = _gated_head_norm(y_ref, z_ref, nw_ref[...])
        x = x + jnp.dot(y, wo_ref[...], preferred_element_type=F32)
    g_ref, wg_ref, wu_ref, wd_ref = refs[:4]
    h = _rms(x, g_ref[...]).astype(BF16)
    gate = jnp.dot(h, wg_ref[...], preferred_element_type=F32)
    up = jnp.dot(h, wu_ref[...], preferred_element_type=F32)
    a = (_silu(gate) * up).astype(BF16)
    y = x + MACARON_WEIGHT * jnp.dot(a, wd_ref[...], preferred_element_type=F32)
    if final:
        y = _rms(y, refs[4][...])
    o_ref[...] = y


def _ffn(x, g, wg, wu, wd, final_g=None, mixer=None, mixer_args=(), tm=512):
    t = x.shape[0]
    row = pl.BlockSpec((tm, D_MODEL), lambda i: (i, 0))
    in_specs, args = [row], [x]
    for a in mixer_args:
        if a.shape[0] == t:
            in_specs.append(pl.BlockSpec((tm, a.shape[1]), lambda i: (i, 0)))
        else:
            in_specs.append(_const_spec(a.shape))
        args.append(a)
    in_specs += [_const_spec((1, D_MODEL)), _const_spec((D_MODEL, D_FF)),
                 _const_spec((D_MODEL, D_FF)), _const_spec((D_FF, D_MODEL))]
    args += [g, wg, wu, wd]
    if final_g is not None:
        in_specs.append(_const_spec((1, D_MODEL)))
        args.append(final_g)
    return pl.pallas_call(
        functools.partial(_ffn_kernel, mixer=mixer, final=final_g is not None),
        grid=(t // tm,), in_specs=in_specs, out_specs=row,
        out_shape=jax.ShapeDtypeStruct((t, D_MODEL), F32),
        compiler_params=_params(1), name="ffn")(*args)


def _a_in_kernel(x_ref, g_ref, w_ref, o_ref):
    h = _rms(x_ref[...], g_ref[...]).astype(BF16)
    o_ref[...] = jnp.dot(h, w_ref[...], preferred_element_type=F32)


def _a_in(x, g, w, tm=512):
    t = x.shape[0]
    return pl.pallas_call(
        _a_in_kernel, grid=(t // tm,),
        in_specs=[pl.BlockSpec((tm, D_MODEL), lambda i: (i, 0)),
                  _const_spec((1, D_MODEL)), _const_spec((D_MODEL, A_IN_WIDTH))],
        out_specs=pl.BlockSpec((tm, A_IN_WIDTH), lambda i: (i, 0)),
        out_shape=jax.ShapeDtypeStruct((t, A_IN_WIDTH), F32),
        compiler_params=_params(1), name="a_in")(x, g, w)


def _t5_bucket_np(distance):
    max_exact = NUM_BUCKETS // 2
    n = distance.astype(np.float32)
    large = np.float32(max_exact) + (
        np.log(np.maximum(n, np.float32(1.0)) / np.float32(max_exact))
        / np.float32(math.log(MAX_DISTANCE / max_exact)) * np.float32(NUM_BUCKETS - max_exact))
    large = np.minimum(large.astype(np.int32), NUM_BUCKETS - 1)
    return np.where(distance < max_exact, distance, large).astype(np.int32)


def _a_bias_rows(rel_bias):
    m = np.arange(2 * A_BLOCK)
    back = A_BLOCK - m
    rows = []
    for g, (window, dilation) in enumerate(A_PATTERNS):
        assert window // dilation == A_BLOCK
        bucket = _t5_bucket_np(np.maximum(back, 0) * dilation)
        b = rel_bias.astype(F32)[:, g * A_HEADS:(g + 1) * A_HEADS][bucket]
        rows.append(jnp.where((back >= 0)[:, None], b, NEG_INF).T)
    return jnp.stack(rows)


def _a_units(q_ref, k_ref, v_ref, bias_scr, g, o_ref, l_ref, units, stride):
    def rows(start, n):
        return pl.ds(start, n) if stride == 1 else pl.ds(start, n, stride=stride)

    head0 = lax.broadcasted_iota(jnp.int32, (1, A_PAIR), 1) < A_HEAD_DIM
    mine = (head0, jnp.logical_not(head0))
    values, scores = [], []
    for base, first in units:
        nk = A_BLOCK if first else 2 * A_BLOCK
        k0 = base if first else base - A_BLOCK * stride
        q = q_ref[rows(base, A_BLOCK), :] * (A_HEAD_DIM ** -0.5)
        k = k_ref[rows(k0, nk), :].astype(BF16)
        values.append(v_ref[rows(k0, nk), :])
        for h in range(2):
            qh = jnp.where(mine[h], q, 0.0).astype(BF16)
            s = lax.dot_general(qh, k, NT_DIMS, preferred_element_type=F32)
            scores.append(s + (bias_scr[g, h, :, A_BLOCK:] if first else bias_scr[g, h]))
    maxes = [jnp.max(s, axis=-1, keepdims=True) for s in scores]
    probs = [jnp.exp(s - m).astype(BF16) for s, m in zip(scores, maxes)]
    results = []
    for n, p in enumerate(probs):
        ve = jnp.where(mine[n % 2], values[n // 2], 1.0).astype(BF16)
        results.append(jnp.dot(p, ve, preferred_element_type=F32))
    for u, (base, _) in enumerate(units):
        r0, r1 = results[2 * u], results[2 * u + 1]
        den = pltpu.roll(jnp.where(head0, r1, r0), A_HEAD_DIM, 1)
        o_ref[rows(base, A_BLOCK), :] = jnp.where(head0, r0, r1) / den
        l_ref[rows(base, A_BLOCK), :] = (jnp.where(head0, maxes[2 * u], maxes[2 * u + 1])
                                         + jnp.log(den))


def _a_attn_kernel(q0, k0, v0, q1, k1, v1, q2, k2, v2, brow_ref, o_ref,
                   bias_scr, sub_scr, o_scr, l_scr, tok_scr):
    pair = pl.program_id(1)
    for g in range(A_GROUPS):
        for h in range(2):
            row = brow_ref[g, pl.ds(2 * pair + h, 1), :]
            bias_scr[g, h] = pltpu.roll(jnp.broadcast_to(row, (A_BLOCK, 2 * A_BLOCK)), 0, 1,
                                        stride=1, stride_axis=0)

    n_blocks = SEQ // A_BLOCK
    quarter = SEQ // 4

    units0 = functools.partial(_a_units, q0, k0, v0, bias_scr, 0, o_scr.at[0], l_scr.at[0], stride=1)
    units0([(u * A_BLOCK, u == 0) for u in range(A_UNROLL)])

    def later_blocks(ib, carry):
        units0([(pl.multiple_of((ib * A_UNROLL + u) * A_BLOCK, A_BLOCK), False)
                for u in range(A_UNROLL)])
        return carry

    if n_blocks > A_UNROLL:
        lax.fori_loop(1, n_blocks // A_UNROLL, later_blocks, 0)

    units1 = functools.partial(_a_units, q1, k1, v1, bias_scr, 1, o_scr.at[1], l_scr.at[1], stride=4)

    per_sub = n_blocks // 4
    subs = max(A_UNROLL // per_sub, 1)

    def subsequence(cb, carry):
        units1([(cb * subs + s + 4 * A_BLOCK * i, i == 0) for s in range(subs) for i in range(per_sub)])
        return carry

    lax.fori_loop(0, 4 // subs, subsequence, 0)

    for r, src in enumerate((q2, k2, v2)):
        for b in range(4):
            sub_scr[r, pl.ds(b * quarter, quarter), :] = src[pl.ds(b, quarter, stride=4), :]
    units2 = functools.partial(_a_units, sub_scr.at[0], sub_scr.at[1], sub_scr.at[2], bias_scr, 2,
                               o_scr.at[2], l_scr.at[2], stride=4)

    rows_per_body = A_UNROLL // 4

    def subsequences(ab, carry):
        units2([(b * quarter + ab * rows_per_body + a, True) for a in range(rows_per_body) for b in range(4)])
        return carry

    lax.fori_loop(0, 4 // rows_per_body, subsequences, 0)
    for n, scr in enumerate((o_scr, l_scr)):
        for b in range(4):
            tok_scr[n, pl.ds(b, quarter, stride=4), :] = scr[2, pl.ds(b * quarter, quarter), :]

    rows_per_step = 256

    def merge(r, carry):
        sl = pl.ds(pl.multiple_of(r * rows_per_step, rows_per_step), rows_per_step)
        l0, l1, l2 = l_scr[0, sl, :], l_scr[1, sl, :], tok_scr[1, sl, :]
        m = jnp.maximum(jnp.maximum(l0, l1), l2)
        w0, w1, w2 = jnp.exp(l0 - m), jnp.exp(l1 - m), jnp.exp(l2 - m)
        num = w0 * o_scr[0, sl, :] + w1 * o_scr[1, sl, :] + w2 * tok_scr[0, sl, :]
        o_ref[sl, :] = (num / (w0 + w1 + w2)).astype(o_ref.dtype)
        return carry

    lax.fori_loop(0, SEQ // rows_per_step, merge, 0)


def _a_attn(proj, bias_rows):
    b = proj.shape[0]
    n_pairs = A_GROUP_WIDTH // A_PAIR
    in_specs = []
    for g in range(A_GROUPS):
        for r in range(3):
            col = (g * 3 + r) * n_pairs
            in_specs.append(pl.BlockSpec((None, SEQ, A_PAIR),
                                         lambda bi, hp, col=col: (bi, 0, col + hp)))
    in_specs.append(_const_spec((A_GROUPS, A_HEADS, 2 * A_BLOCK)))
    return pl.pallas_call(
        _a_attn_kernel, grid=(b, n_pairs), in_specs=in_specs,
        out_specs=pl.BlockSpec((None, SEQ, A_PAIR), lambda bi, hp: (bi, 0, hp)),
        out_shape=jax.ShapeDtypeStruct((b, SEQ, A_GROUP_WIDTH), BF16),
        scratch_shapes=[pltpu.VMEM((A_GROUPS, 2, A_BLOCK, 2 * A_BLOCK), F32),
                        pltpu.VMEM((3, SEQ, A_PAIR), F32),
                        pltpu.VMEM((A_GROUPS, SEQ, A_PAIR), F32),
                        pltpu.VMEM((A_GROUPS, SEQ, A_PAIR), F32),
                        pltpu.VMEM((2, SEQ, A_PAIR), F32)],
        compiler_params=_params(2), name="a_attn")(*([proj] * 9), bias_rows)


B_IN_TILE = 512
B_PREV = 8


def _chunk_cumsum(x, axis):
    pos = lax.broadcasted_iota(jnp.int32, x.shape, axis) & (B_CHUNK - 1)
    step = 1
    while step < B_CHUNK:
        x = x + jnp.where(pos >= step, pltpu.roll(x, step, axis), 0.0)
        step *= 2
    return x


def _b_in_kernel(x_ref, g_ref, wqkv_ref, wz_ref, wab_ref, conv_ref, alog_ref, dtb_ref,
                 qkv_ref, z_ref, gates_ref, pre_scr):
    tiles_per_seq = SEQ // B_IN_TILE
    seq_start = pl.program_id(0) % tiles_per_seq == 0
    h = _rms(x_ref[...], g_ref[...]).astype(BF16)

    @pl.when(seq_start)
    def _():
        pre_scr[pl.ds(0, B_PREV), :] = jnp.zeros((B_PREV, 3 * B_WIDTH), F32)

    @pl.when(jnp.logical_not(seq_start))
    def _():
        pre_scr[pl.ds(0, B_PREV), :] = pre_scr[pl.ds(B_IN_TILE, B_PREV), :]

    pre_scr[pl.ds(B_PREV, B_IN_TILE), :] = jnp.dot(h, wqkv_ref[...], preferred_element_type=F32)
    z_ref[...] = jnp.dot(h, wz_ref[...], preferred_element_type=F32).astype(z_ref.dtype)

    conv = conv_ref[B_CONV - 1:B_CONV, :] * pre_scr[pl.ds(B_PREV, B_IN_TILE), :]
    for j in range(B_CONV - 1):
        conv = conv + conv_ref[j:j + 1, :] * pre_scr[pl.ds(B_PREV - (B_CONV - 1) + j, B_IN_TILE), :]
    act = _silu(conv)
    for hd in range(2 * B_HEADS):
        sl = slice(hd * B_HEAD_DIM, (hd + 1) * B_HEAD_DIM)
        t = act[:, sl]
        n = t * lax.rsqrt(jnp.sum(t * t, axis=-1, keepdims=True) + NORM_EPS)
        if hd < B_HEADS:
            n = n * (B_HEAD_DIM ** -0.5)
        qkv_ref[:, sl] = n.astype(qkv_ref.dtype)
    qkv_ref[:, 2 * B_WIDTH:] = act[:, 2 * B_WIDTH:].astype(qkv_ref.dtype)

    ab = jnp.dot(h, wab_ref[...], preferred_element_type=F32)
    gc = _chunk_cumsum(-jnp.exp(alog_ref[...]) * _softplus(ab + dtb_ref[...]), 0)
    is_decay = lax.broadcasted_iota(jnp.int32, (1, 2 * B_HEADS), 1) < B_HEADS
    gates_ref[...] = jnp.where(is_decay, gc, jax.nn.sigmoid(ab))


def _b_in(x, g, wqkv, wz, wab, conv_w, a_log, dt_bias):
    t = x.shape[0]
    tm = B_IN_TILE
    pad = jnp.zeros((1, B_HEADS), F32)
    a_log = jnp.concatenate([a_log.reshape(1, B_HEADS).astype(F32), pad], axis=1)
    dt_bias = jnp.concatenate([dt_bias.reshape(1, B_HEADS).astype(F32), pad], axis=1)
    return pl.pallas_call(
        _b_in_kernel, grid=(t // tm,),
        in_specs=[pl.BlockSpec((tm, D_MODEL), lambda i: (i, 0)),
                  _const_spec((1, D_MODEL)), _const_spec((D_MODEL, 3 * B_WIDTH)),
                  _const_spec((D_MODEL, B_WIDTH)), _const_spec((D_MODEL, 2 * B_HEADS)),
                  _const_spec((B_CONV, 3 * B_WIDTH)),
                  _const_spec((1, 2 * B_HEADS)), _const_spec((1, 2 * B_HEADS))],
        out_specs=[pl.BlockSpec((tm, 3 * B_WIDTH), lambda i: (i, 0)),
                   pl.BlockSpec((tm, B_WIDTH), lambda i: (i, 0)),
                   pl.BlockSpec((tm, 2 * B_HEADS), lambda i: (i, 0))],
        out_shape=[jax.ShapeDtypeStruct((t, 3 * B_WIDTH), BF16),
                   jax.ShapeDtypeStruct((t, B_WIDTH), BF16),
                   jax.ShapeDtypeStruct((t, 2 * B_HEADS), F32)],
        scratch_shapes=[pltpu.VMEM((B_PREV + tm, 3 * B_WIDTH), F32)],
        compiler_params=pltpu.CompilerParams(dimension_semantics=("arbitrary",),
                                             vmem_limit_bytes=VMEM_LIMIT),
        name="b_in")(x, g, wqkv, wz, wab, conv_w, a_log, dt_bias)


B_STEP_CHUNKS = 8
B_STEP = B_STEP_CHUNKS * B_CHUNK
B_PAIRS = B_HEADS // 2
B_PAIR_W = 2 * B_CHUNK
B_CHUNK_UNROLL = 2
INV_ROWS = 8
XPOSE_ROWS = 8
SCAN_SEQS = 8


def _chunk_rows(c):
    return pl.ds(c, B_CHUNK, stride=B_STEP_CHUNKS)


def _gates_by_head(gates_ref, rows):
    tile = gates_ref[rows, :]
    lanes = 128
    padded = jnp.concatenate([tile, jnp.zeros((B_CHUNK, lanes - tile.shape[1]), F32)], axis=1)
    return padded.T[:tile.shape[1]]


def _chunk_decay(gates_ref, by_head, rows, h, keep):
    gc = gates_ref[rows, h:h + 1]
    return jnp.where(keep, jnp.exp(jnp.minimum(gc - by_head[h:h + 1, :], 0.0)), 0.0)


def _b_low_kernel(k_ref, gates_ref, l_ref, byh_ref):
    upper = (lax.broadcasted_iota(jnp.int32, (B_CHUNK, B_CHUNK), 0)
             < lax.broadcasted_iota(jnp.int32, (B_CHUNK, B_CHUNK), 1))

    def chunk(c, carry):
        rows = pl.ds(pl.multiple_of(c * B_CHUNK, B_CHUNK), B_CHUNK)
        by_head = _gates_by_head(gates_ref, rows)
        byh_ref[c] = by_head
        pair = []
        for h in range(B_HEADS):
            k = k_ref[rows, h * B_HEAD_DIM:(h + 1) * B_HEAD_DIM]
            kk = lax.dot_general(k, k, NT_DIMS, preferred_element_type=F32)
            gc_j = gates_ref[rows, h:h + 1]
            decay = jnp.where(upper, jnp.exp(jnp.minimum(by_head[h:h + 1, :] - gc_j, 0.0)), 0.0)
            pair.append(kk * by_head[B_HEADS + h:B_HEADS + h + 1, :] * decay)
            if h % 2 == 1:
                l_ref[h // 2, _chunk_rows(c), :] = jnp.concatenate(pair, axis=-1)
                pair = []
        return carry

    def chunks(cb, carry):
        for n in range(B_CHUNK_UNROLL):
            chunk(cb * B_CHUNK_UNROLL + n, carry)
        return carry

    lax.fori_loop(0, B_STEP_CHUNKS // B_CHUNK_UNROLL, chunks, 0)


def _b_low(qkv, gates):
    t = qkv.shape[0]
    return pl.pallas_call(
        _b_low_kernel, grid=(t // B_STEP,),
        in_specs=[pl.BlockSpec((B_STEP, B_WIDTH), lambda i: (i, 1)),
                  pl.BlockSpec((B_STEP, 2 * B_HEADS), lambda i: (i, 0))],
        out_specs=[pl.BlockSpec((B_PAIRS, B_STEP, B_PAIR_W), lambda i: (0, i, 0)),
                   pl.BlockSpec((B_STEP_CHUNKS, 2 * B_HEADS, B_CHUNK), lambda i: (i, 0, 0))],
        out_shape=[jax.ShapeDtypeStruct((B_PAIRS, t, B_PAIR_W), F32),
                   jax.ShapeDtypeStruct((t // B_CHUNK, 2 * B_HEADS, B_CHUNK), F32)],
        compiler_params=_params(1), name="b_low")(qkv, gates)


def _b_inv_kernel(l_ref, t_ref, lt_scr, xt_scr):
    steps = SEQ // B_STEP

    def tiles(i):
        return [(pr, pl.ds(pl.multiple_of(s * B_STEP + i * B_STEP_CHUNKS, B_STEP_CHUNKS), B_STEP_CHUNKS))
                for pr in range(B_PAIRS) for s in range(steps)]

    def to_problem_lanes(ib, carry):
        for i in [ib * XPOSE_ROWS + r for r in range(XPOSE_ROWS)]:
            slab_t = jnp.concatenate([l_ref[pr, rows, :] for pr, rows in tiles(i)], axis=0).T
            lt_scr[0, i] = slab_t[:B_CHUNK]
            lt_scr[1, i] = slab_t[B_CHUNK:]
        return carry

    lax.fori_loop(0, B_CHUNK // XPOSE_ROWS, to_problem_lanes, 0)

    xt_scr[...] = jnp.zeros_like(xt_scr)
    sub = lax.broadcasted_iota(jnp.int32, (INV_ROWS, N_CHUNKS * B_PAIRS), 0)
    groups = B_CHUNK // INV_ROWS

    def column_block(hc, carry):
        half, cb = hc // groups, hc % groups
        cols = pl.ds(pl.multiple_of(cb * INV_ROWS, INV_ROWS), INV_ROWS)

        def row_group(ib, carry2):
            i0 = ib * INV_ROWS
            accs = tuple(jnp.where(jnp.logical_and(ib == cb, sub == r), 1.0, 0.0)
                         for r in range(INV_ROWS))

            def earlier_rows(jb, accs):
                for jj in range(INV_ROWS):
                    j = jb * INV_ROWS + jj
                    xj = xt_scr[half, j, cols, :]
                    accs = tuple(accs[r] - lt_scr[half, j, pl.ds(i0 + r, 1), :] * xj
                                 for r in range(INV_ROWS))
                return accs

            accs = list(lax.fori_loop(cb, ib, earlier_rows, accs))
            for r in range(INV_ROWS):
                for r2 in range(r):
                    accs[r] = accs[r] - lt_scr[half, i0 + r2, pl.ds(i0 + r, 1), :] * accs[r2]
                xt_scr[half, i0 + r, cols, :] = accs[r]
            return carry2

        lax.fori_loop(cb, groups, row_group, 0)
        return carry

    lax.fori_loop(0, 2 * groups, column_block, 0)

    def from_problem_lanes(ib, carry):
        for i in [ib * XPOSE_ROWS + r for r in range(XPOSE_ROWS)]:
            slab_t = jnp.concatenate([xt_scr[0, i], xt_scr[1, i]], axis=0).T
            for n, (pr, rows) in enumerate(tiles(i)):
                t_ref[pr, rows, :] = slab_t[n * B_STEP_CHUNKS:(n + 1) * B_STEP_CHUNKS]
        return carry

    lax.fori_loop(0, B_CHUNK // XPOSE_ROWS, from_problem_lanes, 0)


def _b_inv(low):
    t = low.shape[1]
    block = pl.BlockSpec((B_PAIRS, SEQ, B_PAIR_W), lambda b: (0, b, 0))
    scr = pltpu.VMEM((2, B_CHUNK, B_CHUNK, N_CHUNKS * B_PAIRS), F32)
    return pl.pallas_call(
        _b_inv_kernel, grid=(t // SEQ,), in_specs=[block], out_specs=block,
        out_shape=jax.ShapeDtypeStruct(low.shape, F32), scratch_shapes=[scr, scr],
        compiler_params=_params(1), name="b_inv")(low)


def _b_prep_kernel(q_ref, k_ref, v_ref, gates_ref, byh_ref, t_ref,
                   u_ref, w_ref, qd_ref, kt_ref, attn_ref):
    tril = (lax.broadcasted_iota(jnp.int32, (B_CHUNK, B_CHUNK), 0)
            >= lax.broadcasted_iota(jnp.int32, (B_CHUNK, B_CHUNK), 1))

    def chunk(c, carry):
        rows = pl.ds(pl.multiple_of(c * B_CHUNK, B_CHUNK), B_CHUNK)
        last = pl.ds(c * B_CHUNK + B_CHUNK - 1, 1)
        by_head = byh_ref[c]
        pair = []
        for h in range(B_HEADS):
            sl = slice(h * B_HEAD_DIM, (h + 1) * B_HEAD_DIM)
            q, k, v = q_ref[rows, sl], k_ref[rows, sl], v_ref[rows, sl]
            q32, k32, v32 = q.astype(F32), k.astype(F32), v.astype(F32)
            gc = gates_ref[rows, h:h + 1]
            beta = gates_ref[rows, B_HEADS + h:B_HEADS + h + 1]
            egc = jnp.exp(gc)
            kb = k32 * beta
            qk = lax.dot_general(q, k, NT_DIMS, preferred_element_type=F32)
            if h % 2 == 0:
                tinv_pair = t_ref[h // 2, _chunk_rows(c), :]
            tinv = tinv_pair[:, (h % 2) * B_CHUNK:(h % 2 + 1) * B_CHUNK].astype(BF16)
            u_ref[rows, sl] = jnp.dot(tinv, (v32 * beta).astype(BF16),
                                      preferred_element_type=F32).astype(u_ref.dtype)
            w_ref[rows, sl] = jnp.dot(tinv, (kb * egc).astype(BF16),
                                      preferred_element_type=F32).astype(w_ref.dtype)
            qd_ref[rows, sl] = (q32 * egc).astype(qd_ref.dtype)
            kt_ref[rows, sl] = (k32 * jnp.exp(gates_ref[last, h:h + 1] - gc)).astype(kt_ref.dtype)
            pair.append(qk * _chunk_decay(gates_ref, by_head, rows, h, tril))
            if h % 2 == 1:
                attn_ref[rows, (h - 1) * B_CHUNK:(h + 1) * B_CHUNK] = jnp.concatenate(
                    pair, axis=-1).astype(attn_ref.dtype)
                pair = []
        return carry

    lax.fori_loop(0, B_STEP_CHUNKS, chunk, 0)


def _b_prep(qkv, gates, by_head, tinv):
    t = qkv.shape[0]
    tm = B_STEP
    wide = lambda col: pl.BlockSpec((tm, B_WIDTH), lambda i, col=col: (i, col))
    narrow = pl.BlockSpec((tm, B_HEADS * B_CHUNK), lambda i: (i, 0))
    out_wide = jax.ShapeDtypeStruct((t, B_WIDTH), BF16)
    return pl.pallas_call(
        _b_prep_kernel, grid=(t // tm,),
        in_specs=[wide(0), wide(1), wide(2),
                  pl.BlockSpec((tm, 2 * B_HEADS), lambda i: (i, 0)),
                  pl.BlockSpec((B_STEP_CHUNKS, 2 * B_HEADS, B_CHUNK), lambda i: (i, 0, 0)),
                  pl.BlockSpec((B_PAIRS, tm, B_PAIR_W), lambda i: (0, i, 0))],
        out_specs=[wide(0)] * 4 + [narrow],
        out_shape=[out_wide] * 4 + [jax.ShapeDtypeStruct((t, B_HEADS * B_CHUNK), BF16)],
        compiler_params=_params(1), name="b_prep")(qkv, qkv, qkv, gates, by_head, tinv)


def _b_scan_kernel(u_ref, w_ref, qd_ref, kt_ref, attn_ref, gates_ref, o_ref, state):
    @pl.when(pl.program_id(1) == 0)
    def _():
        state[...] = jnp.zeros_like(state)

    zero = jnp.zeros((B_CHUNK, B_HEAD_DIM), BF16)
    seqs = state.shape[0]
    heads = [(b, h) for b in range(seqs) for h in range(B_HEADS)]
    cols = lambda h: slice(h * B_HEAD_DIM, (h + 1) * B_HEAD_DIM)
    ws_qs = {}
    for b, h in heads:
        wq = jnp.concatenate([w_ref[b, :, cols(h)], qd_ref[b, :, cols(h)]], axis=0)
        ws_qs[b, h] = jnp.dot(wq, state[b, h].astype(BF16), preferred_element_type=F32)
    v_new = {}
    for b, h in heads:
        v_new[b, h] = (u_ref[b, :, cols(h)].astype(F32) - ws_qs[b, h][:B_CHUNK]).astype(BF16)
    for b, h in heads:
        chunk_decay = jnp.exp(gates_ref[b, B_CHUNK - 1:B_CHUNK, h:h + 1])
        state[b, h] = state[b, h] * chunk_decay + lax.dot_general(
            kt_ref[b, :, cols(h)], v_new[b, h], TN_DIMS, preferred_element_type=F32)
    for b in range(seqs):
        for hp in range(B_HEADS // 2):
            h0, h1 = 2 * hp, 2 * hp + 1
            vblock = jnp.concatenate(
                [jnp.concatenate([v_new[b, h0], zero], axis=1),
                 jnp.concatenate([zero, v_new[b, h1]], axis=1)], axis=0)
            intra = jnp.dot(attn_ref[b, :, h0 * B_CHUNK:(h1 + 1) * B_CHUNK], vblock,
                            preferred_element_type=F32)
            for j, h in enumerate((h0, h1)):
                o = ws_qs[b, h][B_CHUNK:] + intra[:, j * B_HEAD_DIM:(j + 1) * B_HEAD_DIM]
                o_ref[b, :, cols(h)] = o.astype(o_ref.dtype)


def _b_scan(u, w, qd, kt, attn, gates):
    t = u.shape[0]
    b = t // SEQ
    seqs = math.gcd(b, SCAN_SEQS)
    per_seq = lambda a: a.reshape(b, SEQ, a.shape[-1])
    spec = lambda width: pl.BlockSpec((seqs, B_CHUNK, width), lambda bi, n: (bi, n, 0))
    wide = spec(B_WIDTH)
    y = pl.pallas_call(
        _b_scan_kernel, grid=(b // seqs, N_CHUNKS),
        in_specs=[wide, wide, wide, wide, spec(B_HEADS * B_CHUNK), spec(2 * B_HEADS)],
        out_specs=wide,
        out_shape=jax.ShapeDtypeStruct((b, SEQ, B_WIDTH), BF16),
        scratch_shapes=[pltpu.VMEM((seqs, B_HEADS, B_HEAD_DIM, B_HEAD_DIM), F32)],
        compiler_params=pltpu.CompilerParams(dimension_semantics=("parallel", "arbitrary"),
                                             vmem_limit_bytes=VMEM_LIMIT),
        name="b_scan")(*(per_seq(a) for a in (u, w, qd, kt, attn, gates)))
    return y.reshape(t, B_WIDTH)


def kernel(x, norm_g, ffn_w_gate, ffn_w_up, ffn_w_down, rel_bias, a_w_in, a_w_out,
           b_w_in, b_conv_w, b_a_log, b_dt_bias, b_norm_w, b_w_out, final_g):
    batch, seq, d = x.shape
    assert (seq, d) == (SEQ, D_MODEL)
    depth = norm_g.shape[0]
    t = batch * seq
    x = x.reshape(t, d)
    bias = _a_bias_rows(rel_bias)
    norm_g = norm_g.astype(F32)
    for i in range(depth):
        j = i // 2
        x = _ffn(x, norm_g[i, 0][None], ffn_w_gate[i, 0].astype(BF16), ffn_w_up[i, 0].astype(BF16),
                 ffn_w_down[i, 0].astype(BF16))
        g_mix = norm_g[i, 1][None]
        if i % 2 == 0:
            proj = _a_in(x, g_mix, a_w_in[j].astype(BF16))
            o = _a_attn(proj.reshape(batch, seq, A_IN_WIDTH), bias)
            mixer, mixer_args = "a", (o.reshape(t, A_GROUP_WIDTH), a_w_out[j].astype(BF16))
        else:
            w_in = b_w_in[j].astype(BF16)
            qkv, z, gates = _b_in(x, g_mix, w_in[:, :3 * B_WIDTH], w_in[:, 3 * B_WIDTH:4 * B_WIDTH],
                                  w_in[:, 4 * B_WIDTH:], b_conv_w[j].astype(F32),
                                  b_a_log[j], b_dt_bias[j])
            low, by_head = _b_low(qkv, gates)
            u, w, qd, kt, attn = _b_prep(qkv, gates, by_head, _b_inv(low))
            o = _b_scan(u, w, qd, kt, attn, gates)
            mixer, mixer_args = "b", (o, z, b_norm_w[j].astype(F32)[None], b_w_out[j].astype(BF16))
        last = i == depth - 1
        x = _ffn(x, norm_g[i, 2][None], ffn_w_gate[i, 1].astype(BF16), ffn_w_up[i, 1].astype(BF16),
                 ffn_w_down[i, 1].astype(BF16), final_g=final_g.astype(F32)[None] if last else None,
                 mixer=mixer, mixer_args=mixer_args)
    return x.reshape(batch, seq, d)
```

```python
import functools
import math

import numpy as np
import jax
import jax.numpy as jnp
from jax import lax
from jax.experimental import pallas as pl
from jax.experimental.pallas import tpu as pltpu

F32 = jnp.float32
BF16 = jnp.bfloat16

D_MODEL = 1024
SEQ = 2048
D_FF = 2816
NORM_EPS = 1e-6
MACARON_WEIGHT = 0.5

A_PATTERNS = ((128, 1), (512, 4), (2048, 16))
A_GROUPS = 3
A_HEADS = 8
A_HEAD_DIM = 64
A_GROUP_WIDTH = A_HEADS * A_HEAD_DIM
A_IN_WIDTH = A_GROUPS * 3 * A_GROUP_WIDTH
A_BLOCK = 128
A_PAIR = 2 * A_HEAD_DIM
A_UNROLL = 16
NEG_INF = -1e30
NUM_BUCKETS = 32
MAX_DISTANCE = 2048

B_HEADS = 8
B_HEAD_DIM = 128
B_WIDTH = B_HEADS * B_HEAD_DIM
B_CONV = 4
B_CHUNK = 64
N_CHUNKS = SEQ // B_CHUNK

VMEM_LIMIT = 56 * 1024 * 1024

NT_DIMS = (((1,), (1,)), ((), ()))
TN_DIMS = (((0,), (0,)), ((), ()))


def _const_spec(shape):
    zeros = (0,) * len(shape)
    return pl.BlockSpec(shape, lambda *_: zeros, pipeline_mode=pl.Buffered(1))


def _params(n_axes):
    return pltpu.CompilerParams(dimension_semantics=("parallel",) * n_axes,
                                vmem_limit_bytes=VMEM_LIMIT)


def _rms(x, g):
    ms = jnp.mean(x * x, axis=-1, keepdims=True)
    return x * lax.rsqrt(ms + NORM_EPS) * g


def _silu(x):
    return x * jax.nn.sigmoid(x)


def _softplus(x):
    return jnp.maximum(x, 0.0) + jnp.log1p(jnp.exp(-jnp.abs(x)))


def _gated_head_norm(o_ref, z_ref, nw):
    parts = []
    for h in range(B_HEADS):
        sl = slice(h * B_HEAD_DIM, (h + 1) * B_HEAD_DIM)
        y = _rms(o_ref[:, sl].astype(F32), nw) * _silu(z_ref[:, sl].astype(F32))
        parts.append(y.astype(BF16))
    return jnp.concatenate(parts, axis=1)


def _ffn_kernel(x_ref, *refs, mixer, final):
    refs = list(refs)
    o_ref = refs.pop()
    x = x_ref[...]
    if mixer == "a":
        y_ref, wo_ref = refs[:2]
        refs = refs[2:]
        x = x + jnp.dot(y_ref[...], wo_ref[...], preferred_element_type=F32)
    elif mixer == "b":
        y_ref, z_ref, nw_ref, wo_ref = refs[:4]
        refs = refs[4:]
        y = _gated_head_norm(y_ref, z_ref, nw_ref[...])
        x = x + jnp.dot(y, wo_ref[...], preferred_element_type=F32)
    g_ref, wg_ref, wu_ref, wd_ref = refs[:4]
    h = _rms(x, g_ref[...]).astype(BF16)
    gate = jnp.dot(h, wg_ref[...], preferred_element_type=F32)
    up = jnp.dot(h, wu_ref[...], preferred_element_type=F32)
    a = (_silu(gate) * up).astype(BF16)
    y = x + MACARON_WEIGHT * jnp.dot(a, wd_ref[...], preferred_element_type=F32)
    if final:
        y = _rms(y, refs[4][...])
    o_ref[...] = y


def _ffn(x, g, wg, wu, wd, final_g=None, mixer=None, mixer_args=(), tm=512):
    t = x.shape[0]
    row = pl.BlockSpec((tm, D_MODEL), lambda i: (i, 0))
    in_specs, args = [row], [x]
    for a in mixer_args:
        if a.shape[0] == t:
            in_specs.append(pl.BlockSpec((tm, a.shape[1]), lambda i: (i, 0)))
        else:
            in_specs.append(_const_spec(a.shape))
        args.append(a)
    in_specs += [_const_spec((1, D_MODEL)), _const_spec((D_MODEL, D_FF)),
                 _const_spec((D_MODEL, D_FF)), _const_spec((D_FF, D_MODEL))]
    args += [g, wg, wu, wd]
    if final_g is not None:
        in_specs.append(_const_spec((1, D_MODEL)))
        args.append(final_g)
    return pl.pallas_call(
        functools.partial(_ffn_kernel, mixer=mixer, final=final_g is not None),
        grid=(t // tm,), in_specs=in_specs, out_specs=row,
        out_shape=jax.ShapeDtypeStruct((t, D_MODEL), F32),
        compiler_params=_params(1), name="ffn")(*args)


def _a_in_kernel(x_ref, g_ref, w_ref, o_ref):
    h = _rms(x_ref[...], g_ref[...]).astype(BF16)
    o_ref[...] = jnp.dot(h, w_ref[...], preferred_element_type=F32)


def _a_in(x, g, w, tm=512):
    t = x.shape[0]
    return pl.pallas_call(
        _a_in_kernel, grid=(t // tm,),
        in_specs=[pl.BlockSpec((tm, D_MODEL), lambda i: (i, 0)),
                  _const_spec((1, D_MODEL)), _const_spec((D_MODEL, A_IN_WIDTH))],
        out_specs=pl.BlockSpec((tm, A_IN_WIDTH), lambda i: (i, 0)),
        out_shape=jax.ShapeDtypeStruct((t, A_IN_WIDTH), F32),
        compiler_params=_params(1), name="a_in")(x, g, w)


def _t5_bucket_np(distance):
    max_exact = NUM_BUCKETS // 2
    n = distance.astype(np.float32)
    large = np.float32(max_exact) + (
        np.log(np.maximum(n, np.float32(1.0)) / np.float32(max_exact))
        / np.float32(math.log(MAX_DISTANCE / max_exact)) * np.float32(NUM_BUCKETS - max_exact))
    large = np.minimum(large.astype(np.int32), NUM_BUCKETS - 1)
    return np.where(distance < max_exact, distance, large).astype(np.int32)


def _a_bias_rows(rel_bias):
    m = np.arange(2 * A_BLOCK)
    back = A_BLOCK - m
    rows = []
    for g, (window, dilation) in enumerate(A_PATTERNS):
        assert window // dilation == A_BLOCK
        bucket = _t5_bucket_np(np.maximum(back, 0) * dilation)
        b = rel_bias.astype(F32)[:, g * A_HEADS:(g + 1) * A_HEADS][bucket]
        rows.append(jnp.where((back >= 0)[:, None], b, NEG_INF).T)
    return jnp.stack(rows)


def _a_units(q_ref, k_ref, v_ref, bias_scr, g, o_ref, l_ref, units, stride):
    def rows(start, n):
        return pl.ds(start, n) if stride == 1 else pl.ds(start, n, stride=stride)

    head0 = lax.broadcasted_iota(jnp.int32, (1, A_PAIR), 1) < A_HEAD_DIM
    mine = (head0, jnp.logical_not(head0))
    values, scores = [], []
    for base, first in units:
        nk = A_BLOCK if first else 2 * A_BLOCK
        k0 = base if first else base - A_BLOCK * stride
        q = q_ref[rows(base, A_BLOCK), :] * (A_HEAD_DIM ** -0.5)
        k = k_ref[rows(k0, nk), :].astype(BF16)
        values.append(v_ref[rows(k0, nk), :])
        for h in range(2):
            qh = jnp.where(mine[h], q, 0.0).astype(BF16)
            s = lax.dot_general(qh, k, NT_DIMS, preferred_element_type=F32)
            scores.append(s + (bias_scr[g, h, :, A_BLOCK:] if first else bias_scr[g, h]))
    maxes = [jnp.max(s, axis=-1, keepdims=True) for s in scores]
    probs = [jnp.exp(s - m).astype(BF16) for s, m in zip(scores, maxes)]
    results = []
    for n, p in enumerate(probs):
        ve = jnp.where(mine[n % 2], values[n // 2], 1.0).astype(BF16)
        results.append(jnp.dot(p, ve, preferred_element_type=F32))
    for u, (base, _) in enumerate(units):
        r0, r1 = results[2 * u], results[2 * u + 1]
        den = pltpu.roll(jnp.where(head0, r1, r0), A_HEAD_DIM, 1)
        o_ref[rows(base, A_BLOCK), :] = jnp.where(head0, r0, r1) / den
        l_ref[rows(base, A_BLOCK), :] = (jnp.where(head0, maxes[2 * u], maxes[2 * u + 1])
                                         + jnp.log(den))


def _a_attn_kernel(q0, k0, v0, q1, k1, v1, q2, k2, v2, brow_ref, o_ref,
                   bias_scr, sub_scr, o_scr, l_scr, tok_scr):
    pair = pl.program_id(1)
    for g in range(A_GROUPS):
        for h in range(2):
            row = brow_ref[g, pl.ds(2 * pair + h, 1), :]
            bias_scr[g, h] = pltpu.roll(jnp.broadcast_to(row, (A_BLOCK, 2 * A_BLOCK)), 0, 1,
                                        stride=1, stride_axis=0)

    n_blocks = SEQ // A_BLOCK
    quarter = SEQ // 4

    units0 = functools.partial(_a_units, q0, k0, v0, bias_scr, 0, o_scr.at[0], l_scr.at[0], stride=1)
    units0([(u * A_BLOCK, u == 0) for u in range(A_UNROLL)])

    def later_blocks(ib, carry):
        units0([(pl.multiple_of((ib * A_UNROLL + u) * A_BLOCK, A_BLOCK), False)
                for u in range(A_UNROLL)])
        return carry

    if n_blocks > A_UNROLL:
        lax.fori_loop(1, n_blocks // A_UNROLL, later_blocks, 0)

    units1 = functools.partial(_a_units, q1, k1, v1, bias_scr, 1, o_scr.at[1], l_scr.at[1], stride=4)

    per_sub = n_blocks // 4
    subs = max(A_UNROLL // per_sub, 1)

    def subsequence(cb, carry):
        units1([(cb * subs + s + 4 * A_BLOCK * i, i == 0) for s in range(subs) for i in range(per_sub)])
        return carry

    lax.fori_loop(0, 4 // subs, subsequence, 0)

    for r, src in enumerate((q2, k2, v2)):
        for b in range(4):
            sub_scr[r, pl.ds(b * quarter, quarter), :] = src[pl.ds(b, quarter, stride=4), :]
    units2 = functools.partial(_a_units, sub_scr.at[0], sub_scr.at[1], sub_scr.at[2], bias_scr, 2,
                               o_scr.at[2], l_scr.at[2], stride=4)

    rows_per_body = A_UNROLL // 4

    def subsequences(ab, carry):
        units2([(b * quarter + ab * rows_per_body + a, True) for a in range(rows_per_body) for b in range(4)])
        return carry

    lax.fori_loop(0, 4 // rows_per_body, subsequences, 0)
    for n, scr in enumerate((o_scr, l_scr)):
        for b in range(4):
            tok_scr[n, pl.ds(b, quarter, stride=4), :] = scr[2, pl.ds(b * quarter, quarter), :]

    rows_per_step = 256

    def merge(r, carry):
        sl = pl.ds(pl.multiple_of(r * rows_per_step, rows_per_step), rows_per_step)
        l0, l1, l2 = l_scr[0, sl, :], l_scr[1, sl, :], tok_scr[1, sl, :]
        m = jnp.maximum(jnp.maximum(l0, l1), l2)
        w0, w1, w2 = jnp.exp(l0 - m), jnp.exp(l1 - m), jnp.exp(l2 - m)
        num = w0 * o_scr[0, sl, :] + w1 * o_scr[1, sl, :] + w2 * tok_scr[0, sl, :]
        o_ref[sl, :] = (num / (w0 + w1 + w2)).astype(o_ref.dtype)
        return carry

    lax.fori_loop(0, SEQ // rows_per_step, merge, 0)


def _a_attn(proj, bias_rows):
    b = proj.shape[0]
    n_pairs = A_GROUP_WIDTH // A_PAIR
    in_specs = []
    for g in range(A_GROUPS):
        for r in range(3):
            col = (g * 3 + r) * n_pairs
            in_specs.append(pl.BlockSpec((None, SEQ, A_PAIR),
                                         lambda bi, hp, col=col: (bi, 0, col + hp)))
    in_specs.append(_const_spec((A_GROUPS, A_HEADS, 2 * A_BLOCK)))
    return pl.pallas_call(
        _a_attn_kernel, grid=(b, n_pairs), in_specs=in_specs,
        out_specs=pl.BlockSpec((None, SEQ, A_PAIR), lambda bi, hp: (bi, 0, hp)),
        out_shape=jax.ShapeDtypeStruct((b, SEQ, A_GROUP_WIDTH), BF16),
        scratch_shapes=[pltpu.VMEM((A_GROUPS, 2, A_BLOCK, 2 * A_BLOCK), F32),
                        pltpu.VMEM((3, SEQ, A_PAIR), F32),
                        pltpu.VMEM((A_GROUPS, SEQ, A_PAIR), F32),
                        pltpu.VMEM((A_GROUPS, SEQ, A_PAIR), F32),
                        pltpu.VMEM((2, SEQ, A_PAIR), F32)],
        compiler_params=_params(2), name="a_attn")(*([proj] * 9), bias_rows)


B_IN_TILE = 512
B_PREV = 8


def _chunk_cumsum(x, axis):
    pos = lax.broadcasted_iota(jnp.int32, x.shape, axis) & (B_CHUNK - 1)
    step = 1
    while step < B_CHUNK:
        x = x + jnp.where(pos >= step, pltpu.roll(x, step, axis), 0.0)
        step *= 2
    return x


def _b_in_kernel(x_ref, g_ref, wqkv_ref, wz_ref, wab_ref, conv_ref, alog_ref, dtb_ref,
                 qkv_ref, z_ref, gates_ref, pre_scr):
    tiles_per_seq = SEQ // B_IN_TILE
    seq_start = pl.program_id(0) % tiles_per_seq == 0
    h = _rms(x_ref[...], g_ref[...]).astype(BF16)

    @pl.when(seq_start)
    def _():
        pre_scr[pl.ds(0, B_PREV), :] = jnp.zeros((B_PREV, 3 * B_WIDTH), F32)

    @pl.when(jnp.logical_not(seq_start))
    def _():
        pre_scr[pl.ds(0, B_PREV), :] = pre_scr[pl.ds(B_IN_TILE, B_PREV), :]

    pre_scr[pl.ds(B_PREV, B_IN_TILE), :] = jnp.dot(h, wqkv_ref[...], preferred_element_type=F32)
    z_ref[...] = jnp.dot(h, wz_ref[...], preferred_element_type=F32).astype(z_ref.dtype)

    conv = conv_ref[B_CONV - 1:B_CONV, :] * pre_scr[pl.ds(B_PREV, B_IN_TILE), :]
    for j in range(B_CONV - 1):
        conv = conv + conv_ref[j:j + 1, :] * pre_scr[pl.ds(B_PREV - (B_CONV - 1) + j, B_IN_TILE), :]
    act = _silu(conv)
    for hd in range(2 * B_HEADS):
        sl = slice(hd * B_HEAD_DIM, (hd + 1) * B_HEAD_DIM)
        t = act[:, sl]
        n = t * lax.rsqrt(jnp.sum(t * t, axis=-1, keepdims=True) + NORM_EPS)
        if hd < B_HEADS:
            n = n * (B_HEAD_DIM ** -0.5)
        qkv_ref[:, sl] = n.astype(qkv_ref.dtype)
    qkv_ref[:, 2 * B_WIDTH:] = act[:, 2 * B_WIDTH:].astype(qkv_ref.dtype)

    ab = jnp.dot(h, wab_ref[...], preferred_element_type=F32)
    gc = _chunk_cumsum(-jnp.exp(alog_ref[...]) * _softplus(ab + dtb_ref[...]), 0)
    is_decay = lax.broadcasted_iota(jnp.int32, (1, 2 * B_HEADS), 1) < B_HEADS
    gates_ref[...] = jnp.where(is_decay, gc, jax.nn.sigmoid(ab))


def _b_in(x, g, wqkv, wz, wab, conv_w, a_log, dt_bias):
    t = x.shape[0]
    tm = B_IN_TILE
    pad = jnp.zeros((1, B_HEADS), F32)
    a_log = jnp.concatenate([a_log.reshape(1, B_HEADS).astype(F32), pad], axis=1)
    dt_bias = jnp.concatenate([dt_bias.reshape(1, B_HEADS).astype(F32), pad], axis=1)
    return pl.pallas_call(
        _b_in_kernel, grid=(t // tm,),
        in_specs=[pl.BlockSpec((tm, D_MODEL), lambda i: (i, 0)),
                  _const_spec((1, D_MODEL)), _const_spec((D_MODEL, 3 * B_WIDTH)),
                  _const_spec((D_MODEL, B_WIDTH)), _const_spec((D_MODEL, 2 * B_HEADS)),
                  _const_spec((B_CONV, 3 * B_WIDTH)),
                  _const_spec((1, 2 * B_HEADS)), _const_spec((1, 2 * B_HEADS))],
        out_specs=[pl.BlockSpec((tm, 3 * B_WIDTH), lambda i: (i, 0)),
                   pl.BlockSpec((tm, B_WIDTH), lambda i: (i, 0)),
                   pl.BlockSpec((tm, 2 * B_HEADS), lambda i: (i, 0))],
        out_shape=[jax.ShapeDtypeStruct((t, 3 * B_WIDTH), BF16),
                   jax.ShapeDtypeStruct((t, B_WIDTH), BF16),
                   jax.ShapeDtypeStruct((t, 2 * B_HEADS), F32)],
        scratch_shapes=[pltpu.VMEM((B_PREV + tm, 3 * B_WIDTH), F32)],
        compiler_params=pltpu.CompilerParams(dimension_semantics=("arbitrary",),
                                             vmem_limit_bytes=VMEM_LIMIT),
        name="b_in")(x, g, wqkv, wz, wab, conv_w, a_log, dt_bias)


B_STEP_CHUNKS = 8
B_STEP = B_STEP_CHUNKS * B_CHUNK
B_PAIRS = B_HEADS // 2
B_PAIR_W = 2 * B_CHUNK
B_CHUNK_UNROLL = 2
INV_ROWS = 8
XPOSE_ROWS = 8
SCAN_SEQS = 8


def _chunk_rows(c):
    return pl.ds(c, B_CHUNK, stride=B_STEP_CHUNKS)


def _gates_by_head(gates_ref, rows):
    tile = gates_ref[rows, :]
    lanes = 128
    padded = jnp.concatenate([tile, jnp.zeros((B_CHUNK, lanes - tile.shape[1]), F32)], axis=1)
    return padded.T[:tile.shape[1]]


def _b_low_kernel(k_ref, gates_ref, l_ref, byh_ref):
    upper = (lax.broadcasted_iota(jnp.int32, (B_CHUNK, B_CHUNK), 0)
             < lax.broadcasted_iota(jnp.int32, (B_CHUNK, B_CHUNK), 1))

    def chunk(c, carry):
        rows = pl.ds(pl.multiple_of(c * B_CHUNK, B_CHUNK), B_CHUNK)
        by_head = _gates_by_head(gates_ref, rows)
        byh_ref[c] = by_head
        pair = []
        for h in range(B_HEADS):
            k = k_ref[rows, h * B_HEAD_DIM:(h + 1) * B_HEAD_DIM]
            kk = lax.dot_general(k, k, NT_DIMS, preferred_element_type=F32)
            gc_j = gates_ref[rows, h:h + 1]
            decay = jnp.where(upper, jnp.exp(jnp.minimum(by_head[h:h + 1, :] - gc_j, 0.0)), 0.0)
            pair.append(kk * by_head[B_HEADS + h:B_HEADS + h + 1, :] * decay)
            if h % 2 == 1:
                l_ref[h // 2, _chunk_rows(c), :] = jnp.concatenate(pair, axis=-1)
                pair = []
        return carry

    def chunks(cb, carry):
        for n in range(B_CHUNK_UNROLL):
            chunk(cb * B_CHUNK_UNROLL + n, carry)
        return carry

    lax.fori_loop(0, B_STEP_CHUNKS // B_CHUNK_UNROLL, chunks, 0)


def _b_low(qkv, gates):
    t = qkv.shape[0]
    return pl.pallas_call(
        _b_low_kernel, grid=(t // B_STEP,),
        in_specs=[pl.BlockSpec((B_STEP, B_WIDTH), lambda i: (i, 1)),
                  pl.BlockSpec((B_STEP, 2 * B_HEADS), lambda i: (i, 0))],
        out_specs=[pl.BlockSpec((B_PAIRS, B_STEP, B_PAIR_W), lambda i: (0, i, 0)),
                   pl.BlockSpec((B_STEP_CHUNKS, 2 * B_HEADS, B_CHUNK), lambda i: (i, 0, 0))],
        out_shape=[jax.ShapeDtypeStruct((B_PAIRS, t, B_PAIR_W), F32),
                   jax.ShapeDtypeStruct((t // B_CHUNK, 2 * B_HEADS, B_CHUNK), F32)],
        compiler_params=_params(1), name="b_low")(qkv, gates)


def _b_inv_kernel(l_ref, t_ref, lt_scr, xt_scr):
    steps = SEQ // B_STEP

    def tiles(i):
        return [(pr, pl.ds(pl.multiple_of(s * B_STEP + i * B_STEP_CHUNKS, B_STEP_CHUNKS), B_STEP_CHUNKS))
                for pr in range(B_PAIRS) for s in range(steps)]

    def to_problem_lanes(ib, carry):
        for i in [ib * XPOSE_ROWS + r for r in range(XPOSE_ROWS)]:
            slab_t = jnp.concatenate([l_ref[pr, rows, :] for pr, rows in tiles(i)], axis=0).T
            lt_scr[0, i] = slab_t[:B_CHUNK]
            lt_scr[1, i] = slab_t[B_CHUNK:]
        return carry

    lax.fori_loop(0, B_CHUNK // XPOSE_ROWS, to_problem_lanes, 0)

    xt_scr[...] = jnp.zeros_like(xt_scr)
    sub = lax.broadcasted_iota(jnp.int32, (INV_ROWS, N_CHUNKS * B_PAIRS), 0)
    groups = B_CHUNK // INV_ROWS

    def column_block(hc, carry):
        half, cb = hc // groups, hc % groups
        cols = pl.ds(pl.multiple_of(cb * INV_ROWS, INV_ROWS), INV_ROWS)

        def row_group(ib, carry2):
            i0 = ib * INV_ROWS
            accs = tuple(jnp.where(jnp.logical_and(ib == cb, sub == r), 1.0, 0.0)
                         for r in range(INV_ROWS))

            def earlier_rows(jb, accs):
                for jj in range(INV_ROWS):
                    j = jb * INV_ROWS + jj
                    xj = xt_scr[half, j, cols, :]
                    accs = tuple(accs[r] - lt_scr[half, j, pl.ds(i0 + r, 1), :] * xj
                                 for r in range(INV_ROWS))
                return accs

            accs = list(lax.fori_loop(cb, ib, earlier_rows, accs))
            for r in range(INV_ROWS):
                for r2 in range(r):
                    accs[r] = accs[r] - lt_scr[half, i0 + r2, pl.ds(i0 + r, 1), :] * accs[r2]
                xt_scr[half, i0 + r, cols, :] = accs[r]
            return carry2

        lax.fori_loop(cb, groups, row_group, 0)
        return carry

    lax.fori_loop(0, 2 * groups, column_block, 0)

    def from_problem_lanes(ib, carry):
        for i in [ib * XPOSE_ROWS + r for r in range(XPOSE_ROWS)]:
            slab_t = jnp.concatenate([xt_scr[0, i], xt_scr[1, i]], axis=0).T
            for n, (pr, rows) in enumerate(tiles(i)):
                t_ref[pr, rows, :] = slab_t[n * B_STEP_CHUNKS:(n + 1) * B_STEP_CHUNKS]
        return carry

    lax.fori_loop(0, B_CHUNK // XPOSE_ROWS, from_problem_lanes, 0)


def _b_inv(low):
    t = low.shape[1]
    block = pl.BlockSpec((B_PAIRS, SEQ, B_PAIR_W), lambda b: (0, b, 0))
    scr = pltpu.VMEM((2, B_CHUNK, B_CHUNK, N_CHUNKS * B_PAIRS), F32)
    return pl.pallas_call(
        _b_inv_kernel, grid=(t // SEQ,), in_specs=[block], out_specs=block,
        out_shape=jax.ShapeDtypeStruct(low.shape, F32), scratch_shapes=[scr, scr],
        compiler_params=_params(1), name="b_inv")(low)


def _b_prep_kernel(q_ref, k_ref, v_ref, gates_ref, byh_ref, t_ref,
                   u_ref, w_ref, qd_ref, kt_ref, attn_ref):
    tril = (lax.broadcasted_iota(jnp.int32, (B_CHUNK, B_CHUNK), 0)
            >= lax.broadcasted_iota(jnp.int32, (B_CHUNK, B_CHUNK), 1))

    def chunk(c, carry):
        rows = pl.ds(pl.multiple_of(c * B_CHUNK, B_CHUNK), B_CHUNK)
        last = pl.ds(c * B_CHUNK + B_CHUNK - 1, 1)
        by_head = byh_ref[c]
        pair = []
        for h in range(B_HEADS):
            sl = slice(h * B_HEAD_DIM, (h + 1) * B_HEAD_DIM)
            q, k, v = q_ref[rows, sl], k_ref[rows, sl], v_ref[rows, sl]
            q32, k32, v32 = q.astype(F32), k.astype(F32), v.astype(F32)
            gc = jnp.broadcast_to(gates_ref[rows, h:h + 1], (B_CHUNK, B_HEAD_DIM))
            beta = jnp.broadcast_to(gates_ref[rows, B_HEADS + h:B_HEADS + h + 1], (B_CHUNK, B_HEAD_DIM))
            egc = jnp.exp(gc)
            kb = k32 * beta
            qk = lax.dot_general(q, k, NT_DIMS, preferred_element_type=F32)
            if h % 2 == 0:
                tinv_pair = t_ref[h // 2, _chunk_rows(c), :]
            tinv = tinv_pair[:, (h % 2) * B_CHUNK:(h % 2 + 1) * B_CHUNK].astype(BF16)
            u_ref[rows, sl] = jnp.dot(tinv, (v32 * beta).astype(BF16),
                                      preferred_element_type=F32).astype(u_ref.dtype)
            w_ref[rows, sl] = jnp.dot(tinv, (kb * egc).astype(BF16),
                                      preferred_element_type=F32).astype(w_ref.dtype)
            qd_ref[rows, sl] = (q32 * egc).astype(qd_ref.dtype)
            kt_ref[rows, sl] = (k32 * jnp.exp(gates_ref[last, h:h + 1] - gc)).astype(kt_ref.dtype)
            decay = jnp.where(tril, jnp.exp(jnp.minimum(gc[:, :B_CHUNK] - by_head[h:h + 1, :], 0.0)), 0.0)
            pair.append(qk * decay)
            if h % 2 == 1:
                attn_ref[rows, (h - 1) * B_CHUNK:(h + 1) * B_CHUNK] = jnp.concatenate(
                    pair, axis=-1).astype(attn_ref.dtype)
                pair = []
        return carry

    def chunks(cb, carry):
        for n in range(B_CHUNK_UNROLL):
            chunk(cb * B_CHUNK_UNROLL + n, carry)
        return carry

    lax.fori_loop(0, B_STEP_CHUNKS // B_CHUNK_UNROLL, chunks, 0)


def _b_prep(qkv, gates, by_head, tinv):
    t = qkv.shape[0]
    tm = B_STEP
    wide = lambda col: pl.BlockSpec((tm, B_WIDTH), lambda i, col=col: (i, col))
    narrow = pl.BlockSpec((tm, B_HEADS * B_CHUNK), lambda i: (i, 0))
    out_wide = jax.ShapeDtypeStruct((t, B_WIDTH), BF16)
    return pl.pallas_call(
        _b_prep_kernel, grid=(t // tm,),
        in_specs=[wide(0), wide(1), wide(2),
                  pl.BlockSpec((tm, 2 * B_HEADS), lambda i: (i, 0)),
                  pl.BlockSpec((B_STEP_CHUNKS, 2 * B_HEADS, B_CHUNK), lambda i: (i, 0, 0)),
                  pl.BlockSpec((B_PAIRS, tm, B_PAIR_W), lambda i: (0, i, 0))],
        out_specs=[wide(0)] * 4 + [narrow],
        out_shape=[out_wide] * 4 + [jax.ShapeDtypeStruct((t, B_HEADS * B_CHUNK), BF16)],
        compiler_params=_params(1), name="b_prep")(qkv, qkv, qkv, gates, by_head, tinv)


def _b_scan_kernel(u_ref, w_ref, qd_ref, kt_ref, attn_ref, gates_ref, o_ref, state):
    @pl.when(pl.program_id(1) == 0)
    def _():
        state[...] = jnp.zeros_like(state)

    zero = jnp.zeros((B_CHUNK, B_HEAD_DIM), BF16)
    seqs = state.shape[0]
    heads = [(b, h) for b in range(seqs) for h in range(B_HEADS)]
    cols = lambda h: slice(h * B_HEAD_DIM, (h + 1) * B_HEAD_DIM)
    ws_qs = {}
    for b, h in heads:
        wq = jnp.concatenate([w_ref[b, :, cols(h)], qd_ref[b, :, cols(h)]], axis=0)
        ws_qs[b, h] = jnp.dot(wq, state[b, h].astype(BF16), preferred_element_type=F32)
    v_new = {}
    for b, h in heads:
        v_new[b, h] = (u_ref[b, :, cols(h)].astype(F32) - ws_qs[b, h][:B_CHUNK]).astype(BF16)
    for b, h in heads:
        chunk_decay = jnp.exp(gates_ref[b, B_CHUNK - 1:B_CHUNK, h:h + 1])
        state[b, h] = state[b, h] * chunk_decay + lax.dot_general(
            kt_ref[b, :, cols(h)], v_new[b, h], TN_DIMS, preferred_element_type=F32)
    for b in range(seqs):
        for hp in range(B_HEADS // 2):
            h0, h1 = 2 * hp, 2 * hp + 1
            vblock = jnp.concatenate(
                [jnp.concatenate([v_new[b, h0], zero], axis=1),
                 jnp.concatenate([zero, v_new[b, h1]], axis=1)], axis=0)
            intra = jnp.dot(attn_ref[b, :, h0 * B_CHUNK:(h1 + 1) * B_CHUNK], vblock,
                            preferred_element_type=F32)
            for j, h in enumerate((h0, h1)):
                o = ws_qs[b, h][B_CHUNK:] + intra[:, j * B_HEAD_DIM:(j + 1) * B_HEAD_DIM]
                o_ref[b, :, cols(h)] = o.astype(o_ref.dtype)


def _b_scan(u, w, qd, kt, attn, gates):
    t = u.shape[0]
    b = t // SEQ
    seqs = math.gcd(b, SCAN_SEQS)
    per_seq = lambda a: a.reshape(b, SEQ, a.shape[-1])
    spec = lambda width: pl.BlockSpec((seqs, B_CHUNK, width), lambda bi, n: (bi, n, 0))
    wide = spec(B_WIDTH)
    y = pl.pallas_call(
        _b_scan_kernel, grid=(b // seqs, N_CHUNKS),
        in_specs=[wide, wide, wide, wide, spec(B_HEADS * B_CHUNK), spec(2 * B_HEADS)],
        out_specs=wide,
        out_shape=jax.ShapeDtypeStruct((b, SEQ, B_WIDTH), BF16),
        scratch_shapes=[pltpu.VMEM((seqs, B_HEADS, B_HEAD_DIM, B_HEAD_DIM), F32)],
        compiler_params=pltpu.CompilerParams(dimension_semantics=("parallel", "arbitrary"),
                                             vmem_limit_bytes=VMEM_LIMIT),
        name="b_scan")(*(per_seq(a) for a in (u, w, qd, kt, attn, gates)))
    return y.reshape(t, B_WIDTH)


def kernel(x, norm_g, ffn_w_gate, ffn_w_up, ffn_w_down, rel_bias, a_w_in, a_w_out,
           b_w_in, b_conv_w, b_a_log, b_dt_bias, b_norm_w, b_w_out, final_g):
    batch, seq, d = x.shape
    assert (seq, d) == (SEQ, D_MODEL)
    depth = norm_g.shape[0]
    t = batch * seq
    x = x.reshape(t, d)
    bias = _a_bias_rows(rel_bias)
    norm_g = norm_g.astype(F32)
    for i in range(depth):
        j = i // 2
        x = _ffn(x, norm_g[i, 0][None], ffn_w_gate[i, 0].astype(BF16), ffn_w_up[i, 0].astype(BF16),
                 ffn_w_down[i, 0].astype(BF16))
        g_mix = norm_g[i, 1][None]
        if i % 2 == 0:
            proj = _a_in(x, g_mix, a_w_in[j].astype(BF16))
            o = _a_attn(proj.reshape(batch, seq, A_IN_WIDTH), bias)
            mixer, mixer_args = "a", (o.reshape(t, A_GROUP_WIDTH), a_w_out[j].astype(BF16))
        else:
            w_in = b_w_in[j].astype(BF16)
            qkv, z, gates = _b_in(x, g_mix, w_in[:, :3 * B_WIDTH], w_in[:, 3 * B_WIDTH:4 * B_WIDTH],
                                  w_in[:, 4 * B_WIDTH:], b_conv_w[j].astype(F32),
                                  b_a_log[j], b_dt_bias[j])
            low, by_head = _b_low(qkv, gates)
            u, w, qd, kt, attn = _b_prep(qkv, gates, by_head, _b_inv(low))
            o = _b_scan(u, w, qd, kt, attn, gates)
            mixer, mixer_args = "b", (o, z, b_norm_w[j].astype(F32)[None], b_w_out[j].astype(BF16))
        last = i == depth - 1
        x = _ffn(x, norm_g[i, 2][None], ffn_w_gate[i, 1].astype(BF16), ffn_w_up[i, 1].astype(BF16),
                 ffn_w_down[i, 1].astype(BF16), final_g=final_g.astype(F32)[None] if last else None,
                 mixer=mixer, mixer_args=mixer_args)
    return x.reshape(batch, seq, d)
```

```python
import functools
import math

import numpy as np
import jax
import jax.numpy as jnp
from jax import lax
from jax.experimental import pallas as pl
from jax.experimental.pallas import tpu as pltpu

F32 = jnp.float32
BF16 = jnp.bfloat16

D_MODEL = 1024
SEQ = 2048
D_FF = 2816
NORM_EPS = 1e-6
MACARON_WEIGHT = 0.5

A_PATTERNS = ((128, 1), (512, 4), (2048, 16))
A_GROUPS = 3
A_HEADS = 8
A_HEAD_DIM = 64
A_GROUP_WIDTH = A_HEADS * A_HEAD_DIM
A_IN_WIDTH = A_GROUPS * 3 * A_GROUP_WIDTH
A_BLOCK = 128
A_PAIR = 2 * A_HEAD_DIM
A_UNROLL = 16
NEG_INF = -1e30
NUM_BUCKETS = 32
MAX_DISTANCE = 2048

B_HEADS = 8
B_HEAD_DIM = 128
B_WIDTH = B_HEADS * B_HEAD_DIM
B_CONV = 4
B_CHUNK = 64
N_CHUNKS = SEQ // B_CHUNK

VMEM_LIMIT = 56 * 1024 * 1024

NT_DIMS = (((1,), (1,)), ((), ()))
TN_DIMS = (((0,), (0,)), ((), ()))


def _const_spec(shape):
    zeros = (0,) * len(shape)
    return pl.BlockSpec(shape, lambda *_: zeros, pipeline_mode=pl.Buffered(1))


def _params(n_axes):
    return pltpu.CompilerParams(dimension_semantics=("parallel",) * n_axes,
                                vmem_limit_bytes=VMEM_LIMIT)


def _rms(x, g):
    ms = jnp.mean(x * x, axis=-1, keepdims=True)
    return x * lax.rsqrt(ms + NORM_EPS) * g


def _silu(x):
    return x * jax.nn.sigmoid(x)


def _softplus(x):
    return jnp.maximum(x, 0.0) + jnp.log1p(jnp.exp(-jnp.abs(x)))


def _gated_head_norm(o_ref, z_ref, nw):
    parts = []
    for h in range(B_HEADS):
        sl = slice(h * B_HEAD_DIM, (h + 1) * B_HEAD_DIM)
        y = _rms(o_ref[:, sl].astype(F32), nw) * _silu(z_ref[:, sl].astype(F32))
        parts.append(y.astype(BF16))
    return jnp.concatenate(parts, axis=1)


def _ffn_kernel(x_ref, *refs, mixer, final):
    refs = list(refs)
    o_ref = refs.pop()
    x = x_ref[...]
    if mixer == "a":
        y_ref, wo_ref = refs[:2]
        refs = refs[2:]
        x = x + jnp.dot(y_ref[...], wo_ref[...], preferred_element_type=F32)
    elif mixer == "b":
        y_ref, z_ref, nw_ref, wo_ref = refs[:4]
        refs = refs[4:]
        y = _gated_head_norm(y_ref, z_ref, nw_ref[...])
        x = x + jnp.dot(y, wo_ref[...], preferred_element_type=F32)
    g_ref, wg_ref, wu_ref, wd_ref = refs[:4]
    h = _rms(x, g_ref[...]).astype(BF16)
    gate = jnp.dot(h, wg_ref[...], preferred_element_type=F32)
    up = jnp.dot(h, wu_ref[...], preferred_element_type=F32)
    a = (_silu(gate) * up).astype(BF16)
    y = x + MACARON_WEIGHT * jnp.dot(a, wd_ref[...], preferred_element_type=F32)
    if final:
        y = _rms(y, refs[4][...])
    o_ref[...] = y


def _ffn(x, g, wg, wu, wd, final_g=None, mixer=None, mixer_args=(), tm=512):
    t = x.shape[0]
    row = pl.BlockSpec((tm, D_MODEL), lambda i: (i, 0))
    in_specs, args = [row], [x]
    for a in mixer_args:
        if a.shape[0] == t:
            in_specs.append(pl.BlockSpec((tm, a.shape[1]), lambda i: (i, 0)))
        else:
            in_specs.append(_const_spec(a.shape))
        args.append(a)
    in_specs += [_const_spec((1, D_MODEL)), _const_spec((D_MODEL, D_FF)),
                 _const_spec((D_MODEL, D_FF)), _const_spec((D_FF, D_MODEL))]
    args += [g, wg, wu, wd]
    if final_g is not None:
        in_specs.append(_const_spec((1, D_MODEL)))
        args.append(final_g)
    return pl.pallas_call(
        functools.partial(_ffn_kernel, mixer=mixer, final=final_g is not None),
        grid=(t // tm,), in_specs=in_specs, out_specs=row,
        out_shape=jax.ShapeDtypeStruct((t, D_MODEL), F32),
        compiler_params=_params(1), name="ffn")(*args)


def _a_in_kernel(x_ref, g_ref, w_ref, o_ref):
    h = _rms(x_ref[...], g_ref[...]).astype(BF16)
    o_ref[...] = jnp.dot(h, w_ref[...], preferred_element_type=F32)


def _a_in(x, g, w, tm=512):
    t = x.shape[0]
    return pl.pallas_call(
        _a_in_kernel, grid=(t // tm,),
        in_specs=[pl.BlockSpec((tm, D_MODEL), lambda i: (i, 0)),
                  _const_spec((1, D_MODEL)), _const_spec((D_MODEL, A_IN_WIDTH))],
        out_specs=pl.BlockSpec((tm, A_IN_WIDTH), lambda i: (i, 0)),
        out_shape=jax.ShapeDtypeStruct((t, A_IN_WIDTH), F32),
        compiler_params=_params(1), name="a_in")(x, g, w)


def _t5_bucket_np(distance):
    max_exact = NUM_BUCKETS // 2
    n = distance.astype(np.float32)
    large = np.float32(max_exact) + (
        np.log(np.maximum(n, np.float32(1.0)) / np.float32(max_exact))
        / np.float32(math.log(MAX_DISTANCE / max_exact)) * np.float32(NUM_BUCKETS - max_exact))
    large = np.minimum(large.astype(np.int32), NUM_BUCKETS - 1)
    return np.where(distance < max_exact, distance, large).astype(np.int32)


def _a_bias_rows(rel_bias):
    m = np.arange(2 * A_BLOCK)
    back = A_BLOCK - m
    rows = []
    for g, (window, dilation) in enumerate(A_PATTERNS):
        assert window // dilation == A_BLOCK
        bucket = _t5_bucket_np(np.maximum(back, 0) * dilation)
        b = rel_bias.astype(F32)[:, g * A_HEADS:(g + 1) * A_HEADS][bucket]
        rows.append(jnp.where((back >= 0)[:, None], b, NEG_INF).T)
    return jnp.stack(rows)


def _a_units(q_ref, k_ref, v_ref, bias_scr, g, o_ref, l_ref, units, stride):
    def rows(start, n):
        return pl.ds(start, n) if stride == 1 else pl.ds(start, n, stride=stride)

    head0 = lax.broadcasted_iota(jnp.int32, (1, A_PAIR), 1) < A_HEAD_DIM
    mine = (head0, jnp.logical_not(head0))
    values, scores = [], []
    for base, first in units:
        nk = A_BLOCK if first else 2 * A_BLOCK
        k0 = base if first else base - A_BLOCK * stride
        q = q_ref[rows(base, A_BLOCK), :] * (A_HEAD_DIM ** -0.5)
        k = k_ref[rows(k0, nk), :].astype(BF16)
        values.append(v_ref[rows(k0, nk), :])
        for h in range(2):
            qh = jnp.where(mine[h], q, 0.0).astype(BF16)
            s = lax.dot_general(qh, k, NT_DIMS, preferred_element_type=F32)
            scores.append(s + (bias_scr[g, h, :, A_BLOCK:] if first else bias_scr[g, h]))
    maxes = [jnp.max(s, axis=-1, keepdims=True) for s in scores]
    probs = [jnp.exp(s - m).astype(BF16) for s, m in zip(scores, maxes)]
    results = []
    for n, p in enumerate(probs):
        ve = jnp.where(mine[n % 2], values[n // 2], 1.0).astype(BF16)
        results.append(jnp.dot(p, ve, preferred_element_type=F32))
    for u, (base, _) in enumerate(units):
        r0, r1 = results[2 * u], results[2 * u + 1]
        den = pltpu.roll(jnp.where(head0, r1, r0), A_HEAD_DIM, 1)
        o_ref[rows(base, A_BLOCK), :] = jnp.where(head0, r0, r1) / den
        l_ref[rows(base, A_BLOCK), :] = (jnp.where(head0, maxes[2 * u], maxes[2 * u + 1])
                                         + jnp.log(den))


def _a_attn_kernel(q0, k0, v0, q1, k1, v1, q2, k2, v2, brow_ref, o_ref,
                   bias_scr, sub_scr, o_scr, l_scr, tok_scr):
    pair = pl.program_id(1)
    for g in range(A_GROUPS):
        for h in range(2):
            row = brow_ref[g, pl.ds(2 * pair + h, 1), :]
            bias_scr[g, h] = pltpu.roll(jnp.broadcast_to(row, (A_BLOCK, 2 * A_BLOCK)), 0, 1,
                                        stride=1, stride_axis=0)

    n_blocks = SEQ // A_BLOCK
    quarter = SEQ // 4

    units0 = functools.partial(_a_units, q0, k0, v0, bias_scr, 0, o_scr.at[0], l_scr.at[0], stride=1)
    units0([(u * A_BLOCK, u == 0) for u in range(A_UNROLL)])

    def later_blocks(ib, carry):
        units0([(pl.multiple_of((ib * A_UNROLL + u) * A_BLOCK, A_BLOCK), False)
                for u in range(A_UNROLL)])
        return carry

    if n_blocks > A_UNROLL:
        lax.fori_loop(1, n_blocks // A_UNROLL, later_blocks, 0)

    units1 = functools.partial(_a_units, q1, k1, v1, bias_scr, 1, o_scr.at[1], l_scr.at[1], stride=4)

    per_sub = n_blocks // 4
    subs = max(A_UNROLL // per_sub, 1)

    def subsequence(cb, carry):
        units1([(cb * subs + s + 4 * A_BLOCK * i, i == 0) for s in range(subs) for i in range(per_sub)])
        return carry

    lax.fori_loop(0, 4 // subs, subsequence, 0)

    for r, src in enumerate((q2, k2, v2)):
        for b in range(4):
            sub_scr[r, pl.ds(b * quarter, quarter), :] = src[pl.ds(b, quarter, stride=4), :]
    units2 = functools.partial(_a_units, sub_scr.at[0], sub_scr.at[1], sub_scr.at[2], bias_scr, 2,
                               o_scr.at[2], l_scr.at[2], stride=4)

    rows_per_body = A_UNROLL // 4

    def subsequences(ab, carry):
        units2([(b * quarter + ab * rows_per_body + a, True) for a in range(rows_per_body) for b in range(4)])
        return carry

    lax.fori_loop(0, 4 // rows_per_body, subsequences, 0)
    for n, scr in enumerate((o_scr, l_scr)):
        for b in range(4):
            tok_scr[n, pl.ds(b, quarter, stride=4), :] = scr[2, pl.ds(b * quarter, quarter), :]

    rows_per_step = 256

    def merge(r, carry):
        sl = pl.ds(pl.multiple_of(r * rows_per_step, rows_per_step), rows_per_step)
        l0, l1, l2 = l_scr[0, sl, :], l_scr[1, sl, :], tok_scr[1, sl, :]
        m = jnp.maximum(jnp.maximum(l0, l1), l2)
        w0, w1, w2 = jnp.exp(l0 - m), jnp.exp(l1 - m), jnp.exp(l2 - m)
        num = w0 * o_scr[0, sl, :] + w1 * o_scr[1, sl, :] + w2 * tok_scr[0, sl, :]
        o_ref[sl, :] = (num / (w0 + w1 + w2)).astype(o_ref.dtype)
        return carry

    lax.fori_loop(0, SEQ // rows_per_step, merge, 0)


def _a_attn(proj, bias_rows):
    b = proj.shape[0]
    n_pairs = A_GROUP_WIDTH // A_PAIR
    in_specs = []
    for g in range(A_GROUPS):
        for r in range(3):
            col = (g * 3 + r) * n_pairs
            in_specs.append(pl.BlockSpec((None, SEQ, A_PAIR),
                                         lambda bi, hp, col=col: (bi, 0, col + hp)))
    in_specs.append(_const_spec((A_GROUPS, A_HEADS, 2 * A_BLOCK)))
    return pl.pallas_call(
        _a_attn_kernel, grid=(b, n_pairs), in_specs=in_specs,
        out_specs=pl.BlockSpec((None, SEQ, A_PAIR), lambda bi, hp: (bi, 0, hp)),
        out_shape=jax.ShapeDtypeStruct((b, SEQ, A_GROUP_WIDTH), BF16),
        scratch_shapes=[pltpu.VMEM((A_GROUPS, 2, A_BLOCK, 2 * A_BLOCK), F32),
                        pltpu.VMEM((3, SEQ, A_PAIR), F32),
                        pltpu.VMEM((A_GROUPS, SEQ, A_PAIR), F32),
                        pltpu.VMEM((A_GROUPS, SEQ, A_PAIR), F32),
                        pltpu.VMEM((2, SEQ, A_PAIR), F32)],
        compiler_params=_params(2), name="a_attn")(*([proj] * 9), bias_rows)


B_IN_TILE = 512
B_PREV = 8


def _chunk_cumsum(x, axis):
    pos = lax.broadcasted_iota(jnp.int32, x.shape, axis) & (B_CHUNK - 1)
    step = 1
    while step < B_CHUNK:
        x = x + jnp.where(pos >= step, pltpu.roll(x, step, axis), 0.0)
        step *= 2
    return x


def _b_in_kernel(x_ref, g_ref, w_ref, conv_ref, alog_ref, dtb_ref,
                 qkv_ref, z_ref, gates_ref, pre_scr):
    wqkv_ref = w_ref.at[:, pl.ds(0, 3 * B_WIDTH)]
    wz_ref = w_ref.at[:, pl.ds(3 * B_WIDTH, B_WIDTH)]
    wab_ref = w_ref.at[:, pl.ds(4 * B_WIDTH, 2 * B_HEADS)]
    tiles_per_seq = SEQ // B_IN_TILE
    seq_start = pl.program_id(0) % tiles_per_seq == 0
    h = _rms(x_ref[...], g_ref[...]).astype(BF16)

    @pl.when(seq_start)
    def _():
        pre_scr[pl.ds(0, B_PREV), :] = jnp.zeros((B_PREV, 3 * B_WIDTH), F32)

    @pl.when(jnp.logical_not(seq_start))
    def _():
        pre_scr[pl.ds(0, B_PREV), :] = pre_scr[pl.ds(B_IN_TILE, B_PREV), :]

    pre_scr[pl.ds(B_PREV, B_IN_TILE), :] = jnp.dot(h, wqkv_ref[...], preferred_element_type=F32)
    z_ref[...] = jnp.dot(h, wz_ref[...], preferred_element_type=F32).astype(z_ref.dtype)

    conv = conv_ref[B_CONV - 1:B_CONV, :] * pre_scr[pl.ds(B_PREV, B_IN_TILE), :]
    for j in range(B_CONV - 1):
        conv = conv + conv_ref[j:j + 1, :] * pre_scr[pl.ds(B_PREV - (B_CONV - 1) + j, B_IN_TILE), :]
    act = _silu(conv)
    for hd in range(2 * B_HEADS):
        sl = slice(hd * B_HEAD_DIM, (hd + 1) * B_HEAD_DIM)
        t = act[:, sl]
        n = t * lax.rsqrt(jnp.sum(t * t, axis=-1, keepdims=True) + NORM_EPS)
        if hd < B_HEADS:
            n = n * (B_HEAD_DIM ** -0.5)
        qkv_ref[:, sl] = n.astype(qkv_ref.dtype)
    qkv_ref[:, 2 * B_WIDTH:] = act[:, 2 * B_WIDTH:].astype(qkv_ref.dtype)

    ab = jnp.dot(h, wab_ref[...], preferred_element_type=F32)
    gc = _chunk_cumsum(-jnp.exp(alog_ref[...]) * _softplus(ab + dtb_ref[...]), 0)
    is_decay = lax.broadcasted_iota(jnp.int32, (1, 2 * B_HEADS), 1) < B_HEADS
    gates_ref[...] = jnp.where(is_decay, gc, jax.nn.sigmoid(ab))


def _b_in(x, g, w_in, conv_w, a_log, dt_bias):
    t = x.shape[0]
    tm = B_IN_TILE
    pad = jnp.zeros((1, B_HEADS), F32)
    a_log = jnp.concatenate([a_log.reshape(1, B_HEADS).astype(F32), pad], axis=1)
    dt_bias = jnp.concatenate([dt_bias.reshape(1, B_HEADS).astype(F32), pad], axis=1)
    return pl.pallas_call(
        _b_in_kernel, grid=(t // tm,),
        in_specs=[pl.BlockSpec((tm, D_MODEL), lambda i: (i, 0)),
                  _const_spec((1, D_MODEL)), _const_spec(w_in.shape),
                  _const_spec((B_CONV, 3 * B_WIDTH)),
                  _const_spec((1, 2 * B_HEADS)), _const_spec((1, 2 * B_HEADS))],
        out_specs=[pl.BlockSpec((tm, 3 * B_WIDTH), lambda i: (i, 0)),
                   pl.BlockSpec((tm, B_WIDTH), lambda i: (i, 0)),
                   pl.BlockSpec((tm, 2 * B_HEADS), lambda i: (i, 0))],
        out_shape=[jax.ShapeDtypeStruct((t, 3 * B_WIDTH), BF16),
                   jax.ShapeDtypeStruct((t, B_WIDTH), BF16),
                   jax.ShapeDtypeStruct((t, 2 * B_HEADS), F32)],
        scratch_shapes=[pltpu.VMEM((B_PREV + tm, 3 * B_WIDTH), F32)],
        compiler_params=pltpu.CompilerParams(dimension_semantics=("arbitrary",),
                                             vmem_limit_bytes=VMEM_LIMIT),
        name="b_in")(x, g, w_in, conv_w, a_log, dt_bias)


B_STEP_CHUNKS = 8
B_STEP = B_STEP_CHUNKS * B_CHUNK
B_PAIRS = B_HEADS // 2
B_PAIR_W = 2 * B_CHUNK
B_CHUNK_UNROLL = 2
INV_ROWS = 8
XPOSE_ROWS = 8
SCAN_SEQS = 8


def _chunk_rows(c):
    return pl.ds(c, B_CHUNK, stride=B_STEP_CHUNKS)


def _gates_by_head(gates_ref, rows):
    tile = gates_ref[rows, :]
    lanes = 128
    padded = jnp.concatenate([tile, jnp.zeros((B_CHUNK, lanes - tile.shape[1]), F32)], axis=1)
    return padded.T[:tile.shape[1]]


def _b_low_kernel(k_ref, gates_ref, l_ref, byh_ref):
    upper = (lax.broadcasted_iota(jnp.int32, (B_CHUNK, B_CHUNK), 0)
             < lax.broadcasted_iota(jnp.int32, (B_CHUNK, B_CHUNK), 1))

    def chunk(c, carry):
        rows = pl.ds(pl.multiple_of(c * B_CHUNK, B_CHUNK), B_CHUNK)
        by_head = _gates_by_head(gates_ref, rows)
        byh_ref[c] = by_head
        pair = []
        for h in range(B_HEADS):
            k = k_ref[rows, h * B_HEAD_DIM:(h + 1) * B_HEAD_DIM]
            kk = lax.dot_general(k, k, NT_DIMS, preferred_element_type=F32)
            gc_j = gates_ref[rows, h:h + 1]
            decay = jnp.where(upper, jnp.exp(jnp.minimum(by_head[h:h + 1, :] - gc_j, 0.0)), 0.0)
            pair.append(kk * by_head[B_HEADS + h:B_HEADS + h + 1, :] * decay)
            if h % 2 == 1:
                l_ref[h // 2, _chunk_rows(c), :] = jnp.concatenate(pair, axis=-1)
                pair = []
        return carry

    def chunks(cb, carry):
        for n in range(B_CHUNK_UNROLL):
            chunk(cb * B_CHUNK_UNROLL + n, carry)
        return carry

    lax.fori_loop(0, B_STEP_CHUNKS // B_CHUNK_UNROLL, chunks, 0)


def _b_low(qkv, gates):
    t = qkv.shape[0]
    return pl.pallas_call(
        _b_low_kernel, grid=(t // B_STEP,),
        in_specs=[pl.BlockSpec((B_STEP, B_WIDTH), lambda i: (i, 1)),
                  pl.BlockSpec((B_STEP, 2 * B_HEADS), lambda i: (i, 0))],
        out_specs=[pl.BlockSpec((B_PAIRS, B_STEP, B_PAIR_W), lambda i: (0, i, 0)),
                   pl.BlockSpec((B_STEP_CHUNKS, 2 * B_HEADS, B_CHUNK), lambda i: (i, 0, 0))],
        out_shape=[jax.ShapeDtypeStruct((B_PAIRS, t, B_PAIR_W), F32),
                   jax.ShapeDtypeStruct((t // B_CHUNK, 2 * B_HEADS, B_CHUNK), F32)],
        compiler_params=_params(1), name="b_low")(qkv, gates)


def _b_inv_kernel(l_ref, t_ref, lt_scr, xt_scr):
    steps = SEQ // B_STEP

    def tiles(i):
        return [(pr, pl.ds(pl.multiple_of(s * B_STEP + i * B_STEP_CHUNKS, B_STEP_CHUNKS), B_STEP_CHUNKS))
                for pr in range(B_PAIRS) for s in range(steps)]

    def to_problem_lanes(ib, carry):
        for i in [ib * XPOSE_ROWS + r for r in range(XPOSE_ROWS)]:
            slab_t = jnp.concatenate([l_ref[pr, rows, :] for pr, rows in tiles(i)], axis=0).T
            lt_scr[0, i] = slab_t[:B_CHUNK]
            lt_scr[1, i] = slab_t[B_CHUNK:]
        return carry

    lax.fori_loop(0, B_CHUNK // XPOSE_ROWS, to_problem_lanes, 0)

    xt_scr[...] = jnp.zeros_like(xt_scr)
    sub = lax.broadcasted_iota(jnp.int32, (INV_ROWS, N_CHUNKS * B_PAIRS), 0)
    groups = B_CHUNK // INV_ROWS

    def column_block(hc, carry):
        half, cb = hc // groups, hc % groups
        cols = pl.ds(pl.multiple_of(cb * INV_ROWS, INV_ROWS), INV_ROWS)

        def row_group(ib, carry2):
            i0 = ib * INV_ROWS
            accs = tuple(jnp.where(jnp.logical_and(ib == cb, sub == r), 1.0, 0.0)
                         for r in range(INV_ROWS))

            def earlier_rows(jb, accs):
                for jj in range(INV_ROWS):
                    j = jb * INV_ROWS + jj
                    xj = xt_scr[half, j, cols, :]
                    accs = tuple(accs[r] - lt_scr[half, j, pl.ds(i0 + r, 1), :] * xj
                                 for r in range(INV_ROWS))
                return accs

            accs = list(lax.fori_loop(cb, ib, earlier_rows, accs))
            for r in range(INV_ROWS):
                for r2 in range(r):
                    accs[r] = accs[r] - lt_scr[half, i0 + r2, pl.ds(i0 + r, 1), :] * accs[r2]
                xt_scr[half, i0 + r, cols, :] = accs[r]
            return carry2

        lax.fori_loop(cb, groups, row_group, 0)
        return carry

    lax.fori_loop(0, 2 * groups, column_block, 0)

    def from_problem_lanes(ib, carry):
        for i in [ib * XPOSE_ROWS + r for r in range(XPOSE_ROWS)]:
            slab_t = jnp.concatenate([xt_scr[0, i], xt_scr[1, i]], axis=0).T
            for n, (pr, rows) in enumerate(tiles(i)):
                t_ref[pr, rows, :] = slab_t[n * B_STEP_CHUNKS:(n + 1) * B_STEP_CHUNKS]
        return carry

    lax.fori_loop(0, B_CHUNK // XPOSE_ROWS, from_problem_lanes, 0)


def _b_inv(low):
    t = low.shape[1]
    block = pl.BlockSpec((B_PAIRS, SEQ, B_PAIR_W), lambda b: (0, b, 0))
    scr = pltpu.VMEM((2, B_CHUNK, B_CHUNK, N_CHUNKS * B_PAIRS), F32)
    return pl.pallas_call(
        _b_inv_kernel, grid=(t // SEQ,), in_specs=[block], out_specs=block,
        out_shape=jax.ShapeDtypeStruct(low.shape, F32), scratch_shapes=[scr, scr],
        compiler_params=_params(1), name="b_inv")(low)


def _b_prep_kernel(q_ref, k_ref, v_ref, gates_ref, byh_ref, t_ref,
                   u_ref, w_ref, qd_ref, kt_ref, attn_ref):
    tril = (lax.broadcasted_iota(jnp.int32, (B_CHUNK, B_CHUNK), 0)
            >= lax.broadcasted_iota(jnp.int32, (B_CHUNK, B_CHUNK), 1))

    def chunk(c, carry):
        rows = pl.ds(pl.multiple_of(c * B_CHUNK, B_CHUNK), B_CHUNK)
        last = pl.ds(c * B_CHUNK + B_CHUNK - 1, 1)
        by_head = byh_ref[c]
        pair = []
        for h in range(B_HEADS):
            sl = slice(h * B_HEAD_DIM, (h + 1) * B_HEAD_DIM)
            q, k, v = q_ref[rows, sl], k_ref[rows, sl], v_ref[rows, sl]
            q32, k32, v32 = q.astype(F32), k.astype(F32), v.astype(F32)
            gc = jnp.broadcast_to(gates_ref[rows, h:h + 1], (B_CHUNK, B_HEAD_DIM))
            beta = jnp.broadcast_to(gates_ref[rows, B_HEADS + h:B_HEADS + h + 1], (B_CHUNK, B_HEAD_DIM))
            egc = jnp.exp(gc)
            kb = k32 * beta
            qk = lax.dot_general(q, k, NT_DIMS, preferred_element_type=F32)
            if h % 2 == 0:
                tinv_pair = t_ref[h // 2, _chunk_rows(c), :]
            tinv = tinv_pair[:, (h % 2) * B_CHUNK:(h % 2 + 1) * B_CHUNK].astype(BF16)
            u_ref[rows, sl] = jnp.dot(tinv, (v32 * beta).astype(BF16),
                                      preferred_element_type=F32).astype(u_ref.dtype)
            w_ref[rows, sl] = jnp.dot(tinv, (kb * egc).astype(BF16),
                                      preferred_element_type=F32).astype(w_ref.dtype)
            qd_ref[rows, sl] = (q32 * egc).astype(qd_ref.dtype)
            kt_ref[rows, sl] = (k32 * jnp.exp(gates_ref[last, h:h + 1] - gc)).astype(kt_ref.dtype)
            decay = jnp.where(tril, jnp.exp(jnp.minimum(gc[:, :B_CHUNK] - by_head[h:h + 1, :], 0.0)), 0.0)
            pair.append(qk * decay)
            if h % 2 == 1:
                attn_ref[rows, (h - 1) * B_CHUNK:(h + 1) * B_CHUNK] = jnp.concatenate(
                    pair, axis=-1).astype(attn_ref.dtype)
                pair = []
        return carry

    def chunks(cb, carry):
        for n in range(B_CHUNK_UNROLL):
            chunk(cb * B_CHUNK_UNROLL + n, carry)
        return carry

    lax.fori_loop(0, B_STEP_CHUNKS // B_CHUNK_UNROLL, chunks, 0)


def _b_prep(qkv, gates, by_head, tinv):
    t = qkv.shape[0]
    tm = B_STEP
    wide = lambda col: pl.BlockSpec((tm, B_WIDTH), lambda i, col=col: (i, col))
    narrow = pl.BlockSpec((tm, B_HEADS * B_CHUNK), lambda i: (i, 0))
    out_wide = jax.ShapeDtypeStruct((t, B_WIDTH), BF16)
    return pl.pallas_call(
        _b_prep_kernel, grid=(t // tm,),
        in_specs=[wide(0), wide(1), wide(2),
                  pl.BlockSpec((tm, 2 * B_HEADS), lambda i: (i, 0)),
                  pl.BlockSpec((B_STEP_CHUNKS, 2 * B_HEADS, B_CHUNK), lambda i: (i, 0, 0)),
                  pl.BlockSpec((B_PAIRS, tm, B_PAIR_W), lambda i: (0, i, 0))],
        out_specs=[wide(0)] * 4 + [narrow],
        out_shape=[out_wide] * 4 + [jax.ShapeDtypeStruct((t, B_HEADS * B_CHUNK), BF16)],
        compiler_params=_params(1), name="b_prep")(qkv, qkv, qkv, gates, by_head, tinv)


def _b_scan_kernel(u_ref, w_ref, qd_ref, kt_ref, attn_ref, gates_ref, o_ref, state):
    @pl.when(pl.program_id(1) == 0)
    def _():
        state[...] = jnp.zeros_like(state)

    zero = jnp.zeros((B_CHUNK, B_HEAD_DIM), BF16)
    seqs = state.shape[0]
    heads = [(b, h) for b in range(seqs) for h in range(B_HEADS)]
    cols = lambda h: slice(h * B_HEAD_DIM, (h + 1) * B_HEAD_DIM)
    ws_qs = {}
    for b, h in heads:
        wq = jnp.concatenate([w_ref[b, :, cols(h)], qd_ref[b, :, cols(h)]], axis=0)
        ws_qs[b, h] = jnp.dot(wq, state[b, h].astype(BF16), preferred_element_type=F32)
    v_new = {}
    for b, h in heads:
        v_new[b, h] = (u_ref[b, :, cols(h)].astype(F32) - ws_qs[b, h][:B_CHUNK]).astype(BF16)
    for b, h in heads:
        chunk_decay = jnp.exp(gates_ref[b, B_CHUNK - 1:B_CHUNK, h:h + 1])
        state[b, h] = state[b, h] * chunk_decay + lax.dot_general(
            kt_ref[b, :, cols(h)], v_new[b, h], TN_DIMS, preferred_element_type=F32)
    for b in range(seqs):
        for hp in range(B_HEADS // 2):
            h0, h1 = 2 * hp, 2 * hp + 1
            vblock = jnp.concatenate(
                [jnp.concatenate([v_new[b, h0], zero], axis=1),
                 jnp.concatenate([zero, v_new[b, h1]], axis=1)], axis=0)
            intra = jnp.dot(attn_ref[b, :, h0 * B_CHUNK:(h1 + 1) * B_CHUNK], vblock,
                            preferred_element_type=F32)
            for j, h in enumerate((h0, h1)):
                o = ws_qs[b, h][B_CHUNK:] + intra[:, j * B_HEAD_DIM:(j + 1) * B_HEAD_DIM]
                o_ref[b, :, cols(h)] = o.astype(o_ref.dtype)


def _b_scan(u, w, qd, kt, attn, gates):
    t = u.shape[0]
    b = t // SEQ
    seqs = math.gcd(b, SCAN_SEQS)
    per_seq = lambda a: a.reshape(b, SEQ, a.shape[-1])
    spec = lambda width: pl.BlockSpec((seqs, B_CHUNK, width), lambda bi, n: (bi, n, 0))
    wide = spec(B_WIDTH)
    y = pl.pallas_call(
        _b_scan_kernel, grid=(b // seqs, N_CHUNKS),
        in_specs=[wide, wide, wide, wide, spec(B_HEADS * B_CHUNK), spec(2 * B_HEADS)],
        out_specs=wide,
        out_shape=jax.ShapeDtypeStruct((b, SEQ, B_WIDTH), BF16),
        scratch_shapes=[pltpu.VMEM((seqs, B_HEADS, B_HEAD_DIM, B_HEAD_DIM), F32)],
        compiler_params=pltpu.CompilerParams(dimension_semantics=("parallel", "arbitrary"),
                                             vmem_limit_bytes=VMEM_LIMIT),
        name="b_scan")(*(per_seq(a) for a in (u, w, qd, kt, attn, gates)))
    return y.reshape(t, B_WIDTH)


def kernel(x, norm_g, ffn_w_gate, ffn_w_up, ffn_w_down, rel_bias, a_w_in, a_w_out,
           b_w_in, b_conv_w, b_a_log, b_dt_bias, b_norm_w, b_w_out, final_g):
    batch, seq, d = x.shape
    assert (seq, d) == (SEQ, D_MODEL)
    depth = norm_g.shape[0]
    t = batch * seq
    x = x.reshape(t, d)
    bias = _a_bias_rows(rel_bias)
    norm_g = norm_g.astype(F32)
    for i in range(depth):
        j = i // 2
        x = _ffn(x, norm_g[i, 0][None], ffn_w_gate[i, 0].astype(BF16), ffn_w_up[i, 0].astype(BF16),
                 ffn_w_down[i, 0].astype(BF16))
        g_mix = norm_g[i, 1][None]
        if i % 2 == 0:
            proj = _a_in(x, g_mix, a_w_in[j].astype(BF16))
            o = _a_attn(proj.reshape(batch, seq, A_IN_WIDTH), bias)
            mixer, mixer_args = "a", (o.reshape(t, A_GROUP_WIDTH), a_w_out[j].astype(BF16))
        else:
            qkv, z, gates = _b_in(x, g_mix, b_w_in[j].astype(BF16), b_conv_w[j].astype(F32),
                                  b_a_log[j], b_dt_bias[j])
            low, by_head = _b_low(qkv, gates)
            u, w, qd, kt, attn = _b_prep(qkv, gates, by_head, _b_inv(low))
            o = _b_scan(u, w, qd, kt, attn, gates)
            mixer, mixer_args = "b", (o, z, b_norm_w[j].astype(F32)[None], b_w_out[j].astype(BF16))
        last = i == depth - 1
        x = _ffn(x, norm_g[i, 2][None], ffn_w_gate[i, 1].astype(BF16), ffn_w_up[i, 1].astype(BF16),
                 ffn_w_down[i, 1].astype(BF16), final_g=final_g.astype(F32)[None] if last else None,
                 mixer=mixer, mixer_args=mixer_args)
    return x.reshape(batch, seq, d)
```
